```python
import math
import jax, jax.numpy as jnp
from jax import lax
import numpy as np

D_MODEL = 1024
BATCH = 4
SEQ = 4096
DEPTH = 1
DEC_BATCH = 16
DEC_SEQ = 4096
PAST_LEN = 128

MLA_HEADS = 8
QK_NOPE = 64
QK_ROPE = 32
QK_DIM = QK_NOPE + QK_ROPE
V_HEAD = 64
Q_LORA = 384
KV_LORA = 256
MLA_W = MLA_HEADS * V_HEAD
ROPE_THETA = 10000.0
ATTN_BLOCK = 128
SSM_GROUP = 16
SSM_W = 256
SSM_GROUPS = SSM_W // SSM_GROUP
SSM_STATE = 64
SSM_DT_MIN = 0.001
SSM_DT_MAX = 0.1
MEM_TOKENS = 256
MEM_HEADS = 4
MEM_HD = 64
MEM_W = MEM_HEADS * MEM_HD
N_BRANCH = 3
FF = -(-8 * D_MODEL // (3 * 256)) * 256
EPS = 1e-6
OFF_Q = Q_LORA
OFF_KV = OFF_Q + KV_LORA
OFF_KR = OFF_KV + QK_ROPE
OFF_SSM = OFF_KR + SSM_W
OFF_MEM = OFF_SSM + MEM_W
N_IN = OFF_MEM + N_BRANCH * D_MODEL

kernel_name = "hybrid_mla_s5_memory_encoder"


def _rmsnorm(x, g):
    xf = x.astype(jnp.float32)
    xf = xf * lax.rsqrt(jnp.mean(xf * xf, axis=-1, keepdims=True) + EPS)
    return (xf * g.astype(jnp.float32)).astype(x.dtype)


def _rope_tables(length):
    inv = 1.0 / (ROPE_THETA ** (jnp.arange(0, QK_ROPE, 2, dtype=jnp.float32) / QK_ROPE))
    ang = jnp.arange(length, dtype=jnp.float32)[:, None] * inv[None, :]
    return jnp.cos(ang), jnp.sin(ang)


def _rope(x, cos, sin):
    x1, x2 = jnp.split(x.astype(jnp.float32), 2, axis=-1)
    c = cos[None, :, None, :]
    s = sin[None, :, None, :]
    return jnp.concatenate([x1 * c - x2 * s, x1 * s + x2 * c], axis=-1).astype(x.dtype)


def _blocked_attention(q, k, v, scale):
    b, l, h, dk = q.shape
    nb = l // ATTN_BLOCK
    qb = q.reshape(b, nb, ATTN_BLOCK, h, dk).transpose(1, 0, 2, 3, 4)

    def one_block(qblk):
        s = jnp.einsum('bqhd,bkhd->bhqk', qblk, k, preferred_element_type=jnp.float32) * scale
        p = jax.nn.softmax(s, axis=-1).astype(v.dtype)
        return jnp.einsum('bhqk,bkhd->bqhd', p, v)

    o = lax.map(one_block, qb)
    return o.transpose(1, 0, 2, 3, 4).reshape(b, l, h, v.shape[-1])


def _s5_direction(u, lam_re, lam_im, log_step, b_re, b_im, c_re, c_im, reverse):
    l = u.shape[1]
    dt = jnp.exp(log_step)[:, None]
    mag = jnp.exp(lam_re * dt)
    ab_re = mag * jnp.cos(lam_im * dt)
    ab_im = mag * jnp.sin(lam_im * dt)
    den = lam_re * lam_re + lam_im * lam_im
    n_re = ab_re - 1.0
    z_re = (n_re * lam_re + ab_im * lam_im) / den
    z_im = (ab_im * lam_re - n_re * lam_im) / den
    bb_re = z_re[..., None] * b_re - z_im[..., None] * b_im
    bb_im = z_re[..., None] * b_im + z_im[..., None] * b_re
    bu_re = jnp.einsum('blgh,gph->blgp', u, bb_re)
    bu_im = jnp.einsum('blgh,gph->blgp', u, bb_im)
    a_re = jnp.broadcast_to(ab_re[None, None], (1, l) + ab_re.shape)
    a_im = jnp.broadcast_to(ab_im[None, None], (1, l) + ab_im.shape)

    def combine(e1, e2):
        ar1, ai1, br1, bi1 = e1
        ar2, ai2, br2, bi2 = e2
        return (ar2 * ar1 - ai2 * ai1,
                ar2 * ai1 + ai2 * ar1,
                ar2 * br1 - ai2 * bi1 + br2,
                ar2 * bi1 + ai2 * br1 + bi2)

    _, _, x_re, x_im = lax.associative_scan(combine, (a_re, a_im, bu_re, bu_im), axis=1, reverse=reverse)
    return jnp.einsum('blgp,ghp->blgh', x_re, c_re) - jnp.einsum('blgp,ghp->blgh', x_im, c_im)


def _layer(x, mem, cos, sin, p):
    (norm_mix_g, norm_mem_g, w_in, q_lora_norm_g, kv_lora_norm_g, w_uq, w_ukv, mla_q_norm_g, mla_k_norm_g,
     ssm_lambda_re, ssm_lambda_im, ssm_log_step, ssm_b_re, ssm_b_im, ssm_c_re, ssm_c_im, ssm_d, ssm_w_glu,
     w_mem_kv, mem_q_norm_g, mem_k_norm_g, w_br_attn, w_br_ssm, w_br_mem, w_out,
     norm_ffn_g, w_ffn_gate, w_ffn_up, w_ffn_down) = p
    b, l, d = x.shape
    h = _rmsnorm(x, norm_mix_g)
    z = h @ w_in
    c_q, c_kv, k_rope, u, q_mem, gate_logits = jnp.split(z, (OFF_Q, OFF_KV, OFF_KR, OFF_SSM, OFF_MEM), axis=-1)

    q = (_rmsnorm(c_q, q_lora_norm_g) @ w_uq).reshape(b, l, MLA_HEADS, QK_DIM)
    kv = (_rmsnorm(c_kv, kv_lora_norm_g) @ w_ukv).reshape(b, l, MLA_HEADS, QK_NOPE + V_HEAD)
    k_nope, v = kv[..., :QK_NOPE], kv[..., QK_NOPE:]
    k = jnp.concatenate([k_nope, jnp.broadcast_to(k_rope[:, :, None, :], (b, l, MLA_HEADS, QK_ROPE))], axis=-1)
    q = _rmsnorm(q, mla_q_norm_g)
    k = _rmsnorm(k, mla_k_norm_g)
    q = jnp.concatenate([q[..., :QK_NOPE], _rope(q[..., QK_NOPE:], cos, sin)], axis=-1)
    k = jnp.concatenate([k[..., :QK_NOPE], _rope(k[..., QK_NOPE:], cos, sin)], axis=-1)
    o_attn = _blocked_attention(q, k, v, QK_DIM ** -0.5).reshape(b, l, MLA_W)

    ug = u.reshape(b, l, SSM_GROUPS, SSM_GROUP)
    y_fwd = _s5_direction(ug, ssm_lambda_re[0], ssm_lambda_im[0], ssm_log_step[0],
                          ssm_b_re[0], ssm_b_im[0], ssm_c_re[0], ssm_c_im[0], False)
    y_bwd = _s5_direction(ug, ssm_lambda_re[1], ssm_lambda_im[1], ssm_log_step[1],
                          ssm_b_re[1], ssm_b_im[1], ssm_c_re[1], ssm_c_im[1], True)
    y = (y_fwd + y_bwd).reshape(b, l, SSM_W) + ssm_d * u
    y = jax.nn.gelu(y)
    o_ssm = y * jax.nn.sigmoid(y @ ssm_w_glu)

    m_tok = mem.shape[1]
    mkv = _rmsnorm(mem, norm_mem_g) @ w_mem_kv
    mk = _rmsnorm(mkv[..., :MEM_W].reshape(b, m_tok, MEM_HEADS, MEM_HD), mem_k_norm_g)
    mv = mkv[..., MEM_W:].reshape(b, m_tok, MEM_HEADS, MEM_HD)
    mq = _rmsnorm(q_mem.reshape(b, l, MEM_HEADS, MEM_HD), mem_q_norm_g)
    s = jnp.einsum('blhd,bmhd->bhlm', mq, mk, preferred_element_type=jnp.float32) * (MEM_HD ** -0.5)
    pm = jax.nn.softmax(s, axis=-1).astype(mv.dtype)
    o_mem = jnp.einsum('bhlm,bmhd->blhd', pm, mv).reshape(b, l, MEM_W)

    gates = jax.nn.sigmoid(gate_logits.astype(jnp.float32)).astype(x.dtype).reshape(b, l, N_BRANCH, d)
    merged = (gates[:, :, 0] * (o_attn @ w_br_attn)
              + gates[:, :, 1] * (o_ssm @ w_br_ssm)
              + gates[:, :, 2] * (o_mem @ w_br_mem))
    x = x + merged @ w_out

    h2 = _rmsnorm(x, norm_ffn_g)
    x = x + (jax.nn.silu(h2 @ w_ffn_gate) * (h2 @ w_ffn_up)) @ w_ffn_down
    return x


def setup_inputs(seed: int = 0) -> dict:
    key = jax.random.key(seed)
    ks = jax.random.split(key, 40)
    f32 = jnp.float32

    def nrm(k, shape, scale):
        return jax.random.normal(k, shape, f32) * scale

    def gain(k, shape):
        return 1.0 + 0.02 * jax.random.normal(k, shape, f32)

    n_idx = jnp.arange(SSM_STATE, dtype=f32)
    return {
        "x_prompt": nrm(ks[0], (BATCH, SEQ, D_MODEL), 1.0),
        "x_sample": nrm(ks[1], (DEC_BATCH, DEC_SEQ, D_MODEL), 1.0),
        "mem_prompt": nrm(ks[2], (BATCH, MEM_TOKENS, D_MODEL), 1.0),
        "mem_sample": nrm(ks[3], (DEC_BATCH, MEM_TOKENS, D_MODEL), 1.0),
        "norm_mix_g": gain(ks[4], (DEPTH, D_MODEL)),
        "norm_mem_g": gain(ks[5], (DEPTH, D_MODEL)),
        "w_in": nrm(ks[6], (DEPTH, D_MODEL, N_IN), D_MODEL ** -0.5),
        "q_lora_norm_g": gain(ks[7], (DEPTH, Q_LORA)),
        "kv_lora_norm_g": gain(ks[8], (DEPTH, KV_LORA)),
        "w_uq": nrm(ks[9], (DEPTH, Q_LORA, MLA_HEADS * QK_DIM), Q_LORA ** -0.5),
        "w_ukv": nrm(ks[10], (DEPTH, KV_LORA, MLA_HEADS * (QK_NOPE + V_HEAD)), KV_LORA ** -0.5),
        "mla_q_norm_g": gain(ks[11], (DEPTH, QK_DIM)),
        "mla_k_norm_g": gain(ks[12], (DEPTH, QK_DIM)),
        "ssm_lambda_re": -0.5 + 0.01 * jax.random.normal(ks[13], (DEPTH, 2, SSM_GROUPS, SSM_STATE), f32),
        "ssm_lambda_im": math.pi * n_idx + 0.01 * jax.random.normal(ks[14], (DEPTH, 2, SSM_GROUPS, SSM_STATE), f32),
        "ssm_log_step": jax.random.uniform(ks[15], (DEPTH, 2, SSM_GROUPS), f32,
                                           minval=math.log(SSM_DT_MIN), maxval=math.log(SSM_DT_MAX)),
        "ssm_b_re": nrm(ks[16], (DEPTH, 2, SSM_GROUPS, SSM_STATE, SSM_GROUP), (2 * SSM_GROUP) ** -0.5),
        "ssm_b_im": nrm(ks[17], (DEPTH, 2, SSM_GROUPS, SSM_STATE, SSM_GROUP), (2 * SSM_GROUP) ** -0.5),
        "ssm_c_re": nrm(ks[18], (DEPTH, 2, SSM_GROUPS, SSM_GROUP, SSM_STATE), (2 * SSM_STATE) ** -0.5),
        "ssm_c_im": nrm(ks[19], (DEPTH, 2, SSM_GROUPS, SSM_GROUP, SSM_STATE), (2 * SSM_STATE) ** -0.5),
        "ssm_d": nrm(ks[20], (DEPTH, SSM_W), 1.0),
        "ssm_w_glu": nrm(ks[21], (DEPTH, SSM_W, SSM_W), SSM_W ** -0.5),
        "w_mem_kv": nrm(ks[22], (DEPTH, D_MODEL, 2 * MEM_W), D_MODEL ** -0.5),
        "mem_q_norm_g": gain(ks[23], (DEPTH, MEM_HD)),
        "mem_k_norm_g": gain(ks[24], (DEPTH, MEM_HD)),
        "w_br_attn": nrm(ks[25], (DEPTH, MLA_W, D_MODEL), MLA_W ** -0.5),
        "w_br_ssm": nrm(ks[26], (DEPTH, SSM_W, D_MODEL), SSM_W ** -0.5),
        "w_br_mem": nrm(ks[27], (DEPTH, MEM_W, D_MODEL), MEM_W ** -0.5),
        "w_out": nrm(ks[28], (DEPTH, D_MODEL, D_MODEL), D_MODEL ** -0.5),
        "norm_ffn_g": gain(ks[29], (DEPTH, D_MODEL)),
        "w_ffn_gate": nrm(ks[30], (DEPTH, D_MODEL, FF), D_MODEL ** -0.5),
        "w_ffn_up": nrm(ks[31], (DEPTH, D_MODEL, FF), D_MODEL ** -0.5),
        "w_ffn_down": nrm(ks[32], (DEPTH, FF, D_MODEL), FF ** -0.5),
    }


def reference(x_prompt, x_sample, mem_prompt, mem_sample, norm_mix_g, norm_mem_g, w_in, q_lora_norm_g,
              kv_lora_norm_g, w_uq, w_ukv, mla_q_norm_g, mla_k_norm_g, ssm_lambda_re, ssm_lambda_im,
              ssm_log_step, ssm_b_re, ssm_b_im, ssm_c_re, ssm_c_im, ssm_d, ssm_w_glu, w_mem_kv,
              mem_q_norm_g, mem_k_norm_g, w_br_attn, w_br_ssm, w_br_mem, w_out, norm_ffn_g,
              w_ffn_gate, w_ffn_up, w_ffn_down):
    cos_p, sin_p = _rope_tables(x_prompt.shape[1])
    cos_s, sin_s = _rope_tables(x_sample.shape[1])
    y_prompt = x_prompt
    y_sample = x_sample
    for layer in range(DEPTH):
        p = (norm_mix_g[layer], norm_mem_g[layer], w_in[layer], q_lora_norm_g[layer], kv_lora_norm_g[layer],
             w_uq[layer], w_ukv[layer], mla_q_norm_g[layer], mla_k_norm_g[layer],
             ssm_lambda_re[layer], ssm_lambda_im[layer], ssm_log_step[layer], ssm_b_re[layer], ssm_b_im[layer],
             ssm_c_re[layer], ssm_c_im[layer], ssm_d[layer], ssm_w_glu[layer],
             w_mem_kv[layer], mem_q_norm_g[layer], mem_k_norm_g[layer],
             w_br_attn[layer], w_br_ssm[layer], w_br_mem[layer], w_out[layer],
             norm_ffn_g[layer], w_ffn_gate[layer], w_ffn_up[layer], w_ffn_down[layer])
        y_prompt = _layer(y_prompt, mem_prompt, cos_p, sin_p, p)
        y_sample = _layer(y_sample, mem_sample, cos_s, sin_s, p)
    return (y_prompt, y_sample)
```

```python
import functools
import math

import jax
import jax.numpy as jnp
from jax import lax
from jax.experimental import pallas as pl
from jax.experimental.pallas import tpu as pltpu

F32 = jnp.float32
BF16 = jnp.bfloat16

D_MODEL = 1024
MLA_HEADS = 8
QK_NOPE = 64
QK_ROPE = 32
QK_DIM = QK_NOPE + QK_ROPE
V_HEAD = 64
Q_LORA = 384
KV_LORA = 256
ROPE_THETA = 10000.0
SSM_GROUP = 16
SSM_W = 256
SSM_GROUPS = SSM_W // SSM_GROUP
SSM_STATE = 64
MEM_HEADS = 4
MEM_HD = 64
MEM_W = MEM_HEADS * MEM_HD
N_BRANCH = 3
FF = -(-8 * D_MODEL // (3 * 256)) * 256
EPS = 1e-6
OFF_Q = Q_LORA
OFF_KV = OFF_Q + KV_LORA
OFF_KR = OFF_KV + QK_ROPE
OFF_SSM = OFF_KR + SSM_W
OFF_MEM = OFF_SSM + MEM_W

LANES = 128
HALF_ROPE = QK_ROPE // 2
LOG2E = 1.4426950408889634

Z_CQ = 0
Z_CKV = Z_CQ + Q_LORA
Z_KR = Z_CKV + KV_LORA
Z_U = Z_KR + LANES
Z_QM = Z_U + SSM_W
Z_END = Z_QM + MEM_HEADS * LANES

SSM_T = 32
SSM_TW = SSM_T * SSM_GROUP
SSM_SCAN_STEPS = 7

TOKEN_TILE = 512
ATTN_Q_TILE = 256
VMEM_LIMIT = 52 * 1024 * 1024


def _cparams(*sem):
    return pltpu.CompilerParams(dimension_semantics=sem, vmem_limit_bytes=VMEM_LIMIT)


def _const_spec(shape):
    n = len(shape)
    return pl.BlockSpec(shape, lambda *_: (0,) * n, pipeline_mode=pl.Buffered(1))


def _rms(x, g, n):
    ms = jnp.sum(x * x, axis=-1, keepdims=True) * (1.0 / n)
    return x * lax.rsqrt(ms + EPS) * g


def _sigmoid(x):
    return 1.0 / (1.0 + jnp.exp(-x))


def _dot(a, b):
    return jnp.dot(a, b, preferred_element_type=F32)


def _mem_kv_kernel(mem_ref, g_ref, w_ref, gk_ref, mkt_ref, mv_ref):
    m = _rms(mem_ref[0], g_ref[...], D_MODEL).astype(BF16)
    kv = _dot(m, w_ref[...])
    for h in range(MEM_HEADS):
        k = _rms(kv[:, h * LANES:(h + 1) * LANES], gk_ref[...], MEM_HD)
        mkt_ref[0, h] = k.T.astype(BF16)
        off = (MEM_HEADS + h) * LANES
        mv_ref[0, h] = kv[:, off:off + LANES].astype(BF16)


def _mem_kv(mem, g_mem, w_kv, g_k):
    b, m, _ = mem.shape
    return pl.pallas_call(
        _mem_kv_kernel,
        grid=(b,),
        in_specs=[pl.BlockSpec((1, m, D_MODEL), lambda i: (i, 0, 0)),
                  _const_spec((1, D_MODEL)),
                  _const_spec((D_MODEL, 2 * MEM_HEADS * LANES)),
                  _const_spec((1, LANES))],
        out_specs=[pl.BlockSpec((1, MEM_HEADS, LANES, m), lambda i: (i, 0, 0, 0)),
                   pl.BlockSpec((1, MEM_HEADS, m, LANES), lambda i: (i, 0, 0, 0))],
        out_shape=[jax.ShapeDtypeStruct((b, MEM_HEADS, LANES, m), BF16),
                   jax.ShapeDtypeStruct((b, MEM_HEADS, m, LANES), BF16)],
        compiler_params=_cparams("arbitrary"),
        name="mem_kv",
    )(mem, g_mem, w_kv, g_k)


def _norm_rope(t, g, n_valid, cos_t, sin_hi, sin_lo):
    tn = _rms(t, g, n_valid)
    return (tn * cos_t
            + pltpu.roll(tn, HALF_ROPE, 1) * sin_hi
            + pltpu.roll(tn, LANES - HALF_ROPE, 1) * sin_lo)


def _in_proj_kernel(x_ref, cos_ref, shi_ref, slo_ref, mkt_ref, mv_ref,
                    gmix_ref, w1_ref, gq_ref, wuq_ref, gkv_ref, wukv_ref,
                    gqh_ref, gkh_ref, gmq_ref,
                    q_ref, kt_ref, v_ref, u_ref, om_ref):
    h = _rms(x_ref[0], gmix_ref[...], D_MODEL).astype(BF16)
    z = _dot(h, w1_ref[...])
    u_ref[0] = z[:, Z_U:Z_U + SSM_W]

    cq = _rms(z[:, Z_CQ:Z_CQ + Q_LORA], gq_ref[...], Q_LORA).astype(BF16)
    q_all = _dot(cq, wuq_ref[...])
    ckv = _rms(z[:, Z_CKV:Z_CKV + KV_LORA], gkv_ref[...], KV_LORA).astype(BF16)
    kv_all = _dot(ckv, wukv_ref[...])
    k_rope = z[:, Z_KR:Z_KR + LANES]
    cos_t, sin_hi, sin_lo = cos_ref[...], shi_ref[...], slo_ref[...]
    q_scale = (QK_DIM ** -0.5) * LOG2E

    for hd in range(MLA_HEADS):
        sl = slice(hd * LANES, (hd + 1) * LANES)
        qh = _norm_rope(q_all[:, sl], gqh_ref[...], QK_DIM, cos_t, sin_hi, sin_lo)
        q_ref[0, hd] = (qh * q_scale).astype(BF16)
        kh = _norm_rope(kv_all[:, sl] + k_rope, gkh_ref[...], QK_DIM, cos_t, sin_hi, sin_lo)
        kt_ref[0, hd] = kh.T.astype(BF16)
        off = (MLA_HEADS + hd) * LANES
        v_ref[0, hd] = kv_all[:, off:off + LANES].astype(BF16)

    for hd in range(MEM_HEADS):
        off = Z_QM + hd * LANES
        mq = _rms(z[:, off:off + LANES], gmq_ref[...], MEM_HD) * (MEM_HD ** -0.5)
        s = _dot(mq.astype(BF16), mkt_ref[0, hd])
        p = jnp.exp(s - jnp.max(s, axis=-1, keepdims=True))
        l = jnp.sum(p, axis=-1, keepdims=True)
        o = _dot(p.astype(BF16), mv_ref[0, hd]) * (1.0 / l)
        om_ref[0, :, hd * LANES:(hd + 1) * LANES] = o.astype(BF16)


def _in_proj(x, rope, mkt, mv, wts):
    b, l, _ = x.shape
    tm = TOKEN_TILE
    m = mkt.shape[-1]
    (g_mix, w1, g_q, w_uq, g_kv, w_ukv, g_qh, g_kh, g_mq) = wts
    tile = lambda last: pl.BlockSpec((1, tm, last), lambda i, j: (i, j, 0))
    head_tile = pl.BlockSpec((1, MLA_HEADS, tm, LANES), lambda i, j: (i, 0, j, 0))
    rope_spec = pl.BlockSpec((tm, LANES), lambda i, j: (j, 0))
    return pl.pallas_call(
        _in_proj_kernel,
        grid=(b, l // tm),
        in_specs=[tile(D_MODEL), rope_spec, rope_spec, rope_spec,
                  pl.BlockSpec((1, MEM_HEADS, LANES, m), lambda i, j: (i, 0, 0, 0)),
                  pl.BlockSpec((1, MEM_HEADS, m, LANES), lambda i, j: (i, 0, 0, 0)),
                  _const_spec(g_mix.shape), _const_spec(w1.shape),
                  _const_spec(g_q.shape), _const_spec(w_uq.shape),
                  _const_spec(g_kv.shape), _const_spec(w_ukv.shape),
                  _const_spec(g_qh.shape), _const_spec(g_kh.shape), _const_spec(g_mq.shape)],
        out_specs=[head_tile,
                   pl.BlockSpec((1, MLA_HEADS, LANES, tm), lambda i, j: (i, 0, 0, j)),
                   head_tile, tile(SSM_W), tile(MEM_HEADS * LANES)],
        out_shape=[jax.ShapeDtypeStruct((b, MLA_HEADS, l, LANES), BF16),
                   jax.ShapeDtypeStruct((b, MLA_HEADS, LANES, l), BF16),
                   jax.ShapeDtypeStruct((b, MLA_HEADS, l, LANES), BF16),
                   jax.ShapeDtypeStruct((b, l, SSM_W), F32),
                   jax.ShapeDtypeStruct((b, l, MEM_HEADS * LANES), BF16)],
        compiler_params=_cparams("arbitrary", "arbitrary"),
        name="in_proj",
    )(x, *rope, mkt, mv, g_mix, w1, g_q, w_uq, g_kv, w_ukv, g_qh, g_kh, g_mq)


def _attn_kernel(q_ref, kt_ref, v_ref, o_ref):
    acc = None
    for j in range(2):
        s = _dot(q_ref[0, j], kt_ref[0, j])
        p = jnp.exp2(s - jnp.max(s, axis=-1, keepdims=True))
        l = jnp.sum(p, axis=-1, keepdims=True)
        o = _dot(p.astype(BF16), v_ref[0, j]) * (1.0 / l)
        acc = o if acc is None else acc + o
    o_ref[0] = acc.astype(BF16)


def _attention(q, kt, v):
    b, _, l, _ = q.shape
    tq = ATTN_Q_TILE
    return pl.pallas_call(
        _attn_kernel,
        grid=(b, MLA_HEADS // 2, l // tq),
        in_specs=[pl.BlockSpec((1, 2, tq, LANES), lambda i, h, j: (i, h, j, 0)),
                  pl.BlockSpec((1, 2, LANES, l), lambda i, h, j: (i, h, 0, 0)),
                  pl.BlockSpec((1, 2, l, LANES), lambda i, h, j: (i, h, 0, 0))],
        out_specs=pl.BlockSpec((1, tq, LANES), lambda i, h, j: (i, j, h)),
        out_shape=jax.ShapeDtypeStruct((b, l, MLA_HEADS * V_HEAD), BF16),
        compiler_params=_cparams("arbitrary", "arbitrary", "arbitrary"),
        name="mla_attn",
    )(q, kt, v)


def _cpow(r, th, e):
    mag = jnp.exp(r * e)
    return mag * jnp.cos(th * e), mag * jnp.sin(th * e)


def _zoh(lr, li, dt):
    a_re, a_im = _cpow(lr * dt, li * dt, 1.0)
    den = lr * lr + li * li
    n_re = a_re - 1.0
    return (n_re * lr + a_im * li) / den, (a_im * lr - n_re * li) / den


def _cmul(ar, ai, br, bi):
    return ar * br - ai * bi, ar * bi + ai * br


def _ssm_prep_kernel(prow_ref, pcol_ref, bt_ref, btt_ref, ctt_ref,
                     m_ref, wst_ref, cp_ref, ap_ref):
    t_row = (lax.broadcasted_iota(jnp.int32, (SSM_TW, SSM_STATE), 0) // SSM_GROUP).astype(F32)
    t_lane = (lax.broadcasted_iota(jnp.int32, (SSM_STATE, SSM_TW), 1) // SSM_GROUP).astype(F32)
    lane = lax.broadcasted_iota(jnp.int32, (SSM_GROUP, SSM_TW), 1)
    k_row = lax.broadcasted_iota(jnp.int32, (2 * SSM_SCAN_STEPS + 2, LANES), 0)
    chunk_pow = (SSM_T * jnp.left_shift(1, k_row // 2)).astype(F32)
    odd_row = (k_row % 2) == 1
    lane_ap = lax.broadcasted_iota(jnp.int32, (2 * SSM_SCAN_STEPS + 2, LANES), 1)

    toeplitz = []
    for d in range(2):
        lr2, li2 = prow_ref[0, 3 * d:3 * d + 1, :], prow_ref[0, 3 * d + 1:3 * d + 2, :]
        dt2 = jnp.exp(prow_ref[0, 3 * d + 2:3 * d + 3, :])
        lr, li, dt = lr2[:, :SSM_STATE], li2[:, :SSM_STATE], dt2[:, :SSM_STATE]
        z_re, z_im = _zoh(lr, li, dt)

        p_re, p_im = _cpow(lr2 * dt2, li2 * dt2, chunk_pow)
        ap_ref[0, d] = jnp.where(odd_row, jnp.where(lane_ap < SSM_STATE, -p_im, p_im), p_re)

        e_in = (SSM_T - 1.0) - t_row if d == 0 else t_row
        w_re, w_im = _cmul(*_cpow(lr * dt, li * dt, e_in), z_re, z_im)
        b_re, b_im = btt_ref[0, d, 0], btt_ref[0, d, 1]
        s_re, s_im = _cmul(w_re, w_im, b_re, b_im)
        wst_ref[0, :, 2 * d * SSM_STATE:(2 * d + 1) * SSM_STATE] = s_re.astype(BF16)
        wst_ref[0, :, (2 * d + 1) * SSM_STATE:(2 * d + 2) * SSM_STATE] = s_im.astype(BF16)

        lrc, lic = pcol_ref[0, :, 3 * d:3 * d + 1], pcol_ref[0, :, 3 * d + 1:3 * d + 2]
        dtc = jnp.exp(pcol_ref[0, :, 3 * d + 2:3 * d + 3])
        c_re, c_im = ctt_ref[0, d, 0], ctt_ref[0, d, 1]

        e_out = t_lane + 1.0 if d == 0 else SSM_T - t_lane
        g_re, g_im = _cmul(*_cpow(lrc * dtc, lic * dtc, e_out), c_re, c_im)
        cp_ref[0, 2 * d * SSM_STATE:(2 * d + 1) * SSM_STATE, :] = g_re.astype(BF16)
        cp_ref[0, (2 * d + 1) * SSM_STATE:(2 * d + 2) * SSM_STATE, :] = (-g_im).astype(BF16)

        e_lag = t_lane if d == 0 else (SSM_T - 1.0) - t_lane
        k_re, k_im = _cmul(*_cpow(lrc * dtc, lic * dtc, e_lag), c_re, c_im)
        bb_re, bb_im = _cmul(z_re, z_im, bt_ref[0, d, 0], bt_ref[0, d, 1])
        lag = (jnp.dot(bb_re, k_re, preferred_element_type=F32, precision=lax.Precision.HIGHEST)
               - jnp.dot(bb_im, k_im, preferred_element_type=F32, precision=lax.Precision.HIGHEST))
        toeplitz.append(lag)

    lag_f, lag_b = toeplitz
    for s in range(SSM_T):
        fwd = lag_f if s == 0 else pltpu.roll(lag_f, SSM_GROUP * s, 1)
        fwd = jnp.where(lane >= SSM_GROUP * s, fwd, 0.0)
        shift = (SSM_TW - SSM_GROUP * (SSM_T - 1 - s)) % SSM_TW
        bwd = lag_b if shift == 0 else pltpu.roll(lag_b, shift, 1)
        bwd = jnp.where(lane < SSM_GROUP * (s + 1), bwd, 0.0)
        m_ref[0, s * SSM_GROUP:(s + 1) * SSM_GROUP, :] = (fwd + bwd).astype(BF16)


def _ssm_prep(lam_re, lam_im, log_step, b_re, b_im, c_re, c_im):
    g, p, hg, t = SSM_GROUPS, SSM_STATE, SSM_GROUP, SSM_T
    step = jnp.broadcast_to(log_step[:, :, None], (2, g, p))
    rows = jnp.stack([lam_re[0], lam_im[0], step[0], lam_re[1], lam_im[1], step[1]], axis=1)
    rows = jnp.pad(rows, ((0, 0), (0, 2), (0, 0)))
    prow = jnp.concatenate([rows, rows], axis=-1)
    pcol = rows.transpose(0, 2, 1)
    bt = jnp.stack([b_re, b_im], axis=1).transpose(2, 0, 1, 4, 3)
    btt = jnp.tile(bt, (1, 1, 1, t, 1))
    ct = jnp.stack([c_re, c_im], axis=1).transpose(2, 0, 1, 4, 3)
    ctt = jnp.tile(ct, (1, 1, 1, 1, t))
    n_ap = 2 * SSM_SCAN_STEPS + 2
    blk = lambda *s: pl.BlockSpec((1,) + s, lambda i: (i,) + (0,) * len(s))
    return pl.pallas_call(
        _ssm_prep_kernel,
        grid=(g,),
        in_specs=[blk(8, LANES), blk(p, 8), blk(2, 2, hg, p), blk(2, 2, t * hg, p), blk(2, 2, p, t * hg)],
        out_specs=[blk(SSM_TW, SSM_TW), blk(SSM_TW, 4 * p), blk(4 * p, SSM_TW), blk(2, n_ap, LANES)],
        out_shape=[jax.ShapeDtypeStruct((g, SSM_TW, SSM_TW), BF16),
                   jax.ShapeDtypeStruct((g, SSM_TW, 4 * p), BF16),
                   jax.ShapeDtypeStruct((g, 4 * p, SSM_TW), BF16),
                   jax.ShapeDtypeStruct((g, 2, n_ap, LANES), F32)],
        compiler_params=_cparams("arbitrary"),
        name="ssm_prep",
    )(prow, pcol, bt, btt, ctt)


def _ssm_kernel(u_ref, m_ref, wst_ref, cp_ref, ap_ref, y_ref):
    u = u_ref[0, 0]
    nc = u.shape[0]
    z = _dot(u, wst_ref[0])
    row = lax.broadcasted_iota(jnp.int32, (nc, LANES), 0)
    sf, sb = z[:, :LANES], z[:, LANES:]
    for k in range(nc.bit_length() - 1):
        sh = 1 << k
        t = jnp.where(row >= sh, pltpu.roll(sf, sh, 0), 0.0)
        sf = sf + ap_ref[0, 0, 2 * k:2 * k + 1, :] * t \
            + ap_ref[0, 0, 2 * k + 1:2 * k + 2, :] * pltpu.roll(t, SSM_STATE, 1)
        t = jnp.where(row < nc - sh, pltpu.roll(sb, nc - sh, 0), 0.0)
        sb = sb + ap_ref[0, 1, 2 * k:2 * k + 1, :] * t \
            + ap_ref[0, 1, 2 * k + 1:2 * k + 2, :] * pltpu.roll(t, SSM_STATE, 1)
    s_prev = jnp.where(row >= 1, pltpu.roll(sf, 1, 0), 0.0)
    s_next = jnp.where(row < nc - 1, pltpu.roll(sb, nc - 1, 0), 0.0)
    s = jnp.concatenate([s_prev, s_next], axis=1).astype(BF16)
    y_ref[0, 0] = _dot(u, m_ref[0]) + _dot(s, cp_ref[0])


def _ssm(u_g, tables):
    m, wst, cp, ap = tables
    b, g, nc, tw = u_g.shape
    assert nc & (nc - 1) == 0 and nc <= 1 << SSM_SCAN_STEPS, "chunk count must be a power of two"
    tab = lambda a: pl.BlockSpec((1,) + a.shape[1:], lambda gi, bi: (gi,) + (0,) * (a.ndim - 1))
    io = pl.BlockSpec((1, 1, nc, tw), lambda gi, bi: (bi, gi, 0, 0))
    return pl.pallas_call(
        _ssm_kernel,
        grid=(g, b),
        in_specs=[io, tab(m), tab(wst), tab(cp), tab(ap)],
        out_specs=io,
        out_shape=jax.ShapeDtypeStruct((b, g, nc, tw), F32),
        compiler_params=_cparams("arbitrary", "arbitrary"),
        name="ssm_scan",
    )(u_g, m, wst, cp, ap)


def _gelu_tanh(x):
    return 0.5 * x * (1.0 + jnp.tanh(math.sqrt(2.0 / math.pi) * (x + 0.044715 * (x * x * x))))


def _merge_kernel(x_ref, oa_ref, y_ref, u_ref, om_ref,
                  gmix_ref, wg_ref, d_ref, wglu_ref, wa_ref, ws_ref, wm_ref, wout_ref, o_ref):
    x = x_ref[...]
    h = _rms(x, gmix_ref[...], D_MODEL).astype(BF16)
    y = _gelu_tanh(y_ref[...] + d_ref[...] * u_ref[...])
    o_ssm = (y * _sigmoid(_dot(y.astype(BF16), wglu_ref[...]))).astype(BF16)
    branches = ((oa_ref[...], wa_ref), (o_ssm, ws_ref), (om_ref[...], wm_ref))
    merged = None
    for i, (o_b, w_ref) in enumerate(branches):
        gate = _sigmoid(_dot(h, wg_ref[:, i * D_MODEL:(i + 1) * D_MODEL]))
        term = gate * _dot(o_b, w_ref[...])
        merged = term if merged is None else merged + term
    o_ref[...] = x + _dot(merged.astype(BF16), wout_ref[...])


def _merge(x2, oa2, y2, u2, om2, wts):
    n = x2.shape[0]
    tm = TOKEN_TILE
    tile = lambda a: pl.BlockSpec((tm, a.shape[1]), lambda i: (i, 0))
    return pl.pallas_call(
        _merge_kernel,
        grid=(n // tm,),
        in_specs=[tile(x2), tile(oa2), tile(y2), tile(u2), tile(om2)]
                 + [_const_spec(w.shape) for w in wts],
        out_specs=tile(x2),
        out_shape=jax.ShapeDtypeStruct(x2.shape, F32),
        compiler_params=_cparams("arbitrary"),
        name="merge_out",
    )(x2, oa2, y2, u2, om2, *wts)


def _ffn_kernel(x_ref, g_ref, wgate_ref, wup_ref, wdown_ref, o_ref):
    x = x_ref[...]
    h = _rms(x, g_ref[...], D_MODEL).astype(BF16)
    gate = _dot(h, wgate_ref[...])
    act = (gate * _sigmoid(gate) * _dot(h, wup_ref[...])).astype(BF16)
    o_ref[...] = x + _dot(act, wdown_ref[...])


def _ffn(x2, wts):
    n = x2.shape[0]
    tm = TOKEN_TILE
    tile = pl.BlockSpec((tm, D_MODEL), lambda i: (i, 0))
    return pl.pallas_call(
        _ffn_kernel,
        grid=(n // tm,),
        in_specs=[tile] + [_const_spec(w.shape) for w in wts],
        out_specs=tile,
        out_shape=jax.ShapeDtypeStruct(x2.shape, F32),
        compiler_params=_cparams("arbitrary"),
        name="ffn",
    )(x2, *wts)


def _pad_heads(w, heads, width):
    lead = w.shape[:-1]
    w = w.reshape(lead + (heads, width))
    w = jnp.pad(w, [(0, 0)] * len(lead) + [(0, 0), (0, LANES - width)])
    return w.reshape(lead + (heads * LANES,))


def _pad_row(g):
    return jnp.pad(g, (0, LANES - g.shape[0])).reshape(1, LANES)


def _rope_tables(length):
    inv = 1.0 / (ROPE_THETA ** (jnp.arange(0, QK_ROPE, 2, dtype=F32) / QK_ROPE))
    ang = jnp.arange(length, dtype=F32)[:, None] * inv[None, :]
    cos, sin = jnp.cos(ang), jnp.sin(ang)
    zeros = jnp.zeros_like(cos)
    pad = jnp.zeros((length, LANES - QK_DIM), F32)
    ones = jnp.ones((length, QK_NOPE), F32)
    nope0 = jnp.zeros((length, QK_NOPE), F32)
    cos_t = jnp.concatenate([ones, cos, cos, pad], axis=1)
    sin_hi = jnp.concatenate([nope0, zeros, sin, pad], axis=1)
    sin_lo = jnp.concatenate([nope0, -sin, zeros, pad], axis=1)
    return cos_t, sin_hi, sin_lo


def _layer(x, mem, rope, p):
    b, l, d = x.shape
    mkt, mv = _mem_kv(mem, p["g_mem"], p["w_mem_kv"], p["g_mk"])
    q, kt, v, u, o_mem = _in_proj(x, rope, mkt, mv, p["in_proj"])
    o_attn = _attention(q, kt, v)

    nc = l // SSM_T
    u_g = u.astype(BF16).reshape(b, nc, SSM_T, SSM_GROUPS, SSM_GROUP)
    u_g = u_g.transpose(0, 3, 1, 2, 4).reshape(b, SSM_GROUPS, nc, SSM_TW)
    y_g = _ssm(u_g, p["ssm"])
    y = y_g.reshape(b, SSM_GROUPS, nc, SSM_T, SSM_GROUP).transpose(0, 2, 3, 1, 4).reshape(b * l, SSM_W)

    x2 = x.reshape(b * l, d)
    x1 = _merge(x2, o_attn.reshape(b * l, -1), y, u.reshape(b * l, SSM_W),
                o_mem.reshape(b * l, -1), p["merge"])
    return _ffn(x1, p["ffn"]).reshape(b, l, d)


def kernel(x_prompt, x_sample, mem_prompt, mem_sample, norm_mix_g, norm_mem_g, w_in, q_lora_norm_g, kv_lora_norm_g, w_uq, w_ukv, mla_q_norm_g, mla_k_norm_g, ssm_lambda_re, ssm_lambda_im, ssm_log_step, ssm_b_re, ssm_b_im, ssm_c_re, ssm_c_im, ssm_d, ssm_w_glu, w_mem_kv, mem_q_norm_g, mem_k_norm_g, w_br_attn, w_br_ssm, w_br_mem, w_out, norm_ffn_g, w_ffn_gate, w_ffn_up, w_ffn_down):
    assert norm_mix_g.shape[0] == 1, "single-layer encoder"
    row = lambda g: g.reshape(1, -1)
    w_in0 = w_in[0]

    k_rope_cols = jnp.pad(w_in0[:, OFF_KV:OFF_KR], ((0, 0), (QK_NOPE, LANES - QK_DIM)))
    w1 = jnp.concatenate([w_in0[:, :OFF_KV], k_rope_cols, w_in0[:, OFF_KR:OFF_SSM],
                          _pad_heads(w_in0[:, OFF_SSM:OFF_MEM], MEM_HEADS, MEM_HD)], axis=1).astype(BF16)
    w_uq_p = _pad_heads(w_uq[0], MLA_HEADS, QK_DIM).astype(BF16)
    w_ukv_h = w_ukv[0].reshape(KV_LORA, MLA_HEADS, QK_NOPE + V_HEAD)
    w_k = _pad_heads(w_ukv_h[..., :QK_NOPE].reshape(KV_LORA, -1), MLA_HEADS, QK_NOPE)
    w_v = w_ukv_h[..., QK_NOPE:]
    w_v = jnp.concatenate([w_v, jnp.zeros_like(w_v)], axis=-1)
    w_v = jnp.where((jnp.arange(MLA_HEADS) % 2 == 1)[None, :, None], jnp.roll(w_v, V_HEAD, axis=-1), w_v)
    w_ukv_p = jnp.concatenate([w_k, w_v.reshape(KV_LORA, -1)], axis=1).astype(BF16)

    w_mkv = w_mem_kv[0]
    w_mkv_p = jnp.concatenate([_pad_heads(w_mkv[:, :MEM_W], MEM_HEADS, MEM_HD),
                               _pad_heads(w_mkv[:, MEM_W:], MEM_HEADS, MEM_HD)], axis=1).astype(BF16)
    w_br_mem_p = jnp.pad(w_br_mem[0].reshape(MEM_HEADS, MEM_HD, D_MODEL),
                         ((0, 0), (0, LANES - MEM_HD), (0, 0))).reshape(MEM_HEADS * LANES, D_MODEL)

    params = {
        "g_mem": row(norm_mem_g[0]), "w_mem_kv": w_mkv_p, "g_mk": _pad_row(mem_k_norm_g[0]),
        "in_proj": (row(norm_mix_g[0]), w1, row(q_lora_norm_g[0]), w_uq_p, row(kv_lora_norm_g[0]), w_ukv_p,
                    _pad_row(mla_q_norm_g[0]), _pad_row(mla_k_norm_g[0]), _pad_row(mem_q_norm_g[0])),
        "ssm": _ssm_prep(ssm_lambda_re[0], ssm_lambda_im[0], ssm_log_step[0],
                         ssm_b_re[0], ssm_b_im[0], ssm_c_re[0], ssm_c_im[0]),
        "merge": (row(norm_mix_g[0]), w_in0[:, OFF_MEM:].astype(BF16), row(ssm_d[0]),
                  ssm_w_glu[0].astype(BF16), w_br_attn[0].astype(BF16), w_br_ssm[0].astype(BF16),
                  w_br_mem_p.astype(BF16), w_out[0].astype(BF16)),
        "ffn": (row(norm_ffn_g[0]), w_ffn_gate[0].astype(BF16), w_ffn_up[0].astype(BF16),
                w_ffn_down[0].astype(BF16)),
    }
    rope = _rope_tables(x_prompt.shape[1])
    y_prompt = _layer(x_prompt, mem_prompt, rope, params)
    rope_s = rope if x_sample.shape[1] == x_prompt.shape[1] else _rope_tables(x_sample.shape[1])
    y_sample = _layer(x_sample, mem_sample, rope_s, params)
    return (y_prompt, y_sample)
```

```python
import math

import jax
import jax.numpy as jnp
from jax import lax
from jax.experimental import pallas as pl
from jax.experimental.pallas import tpu as pltpu

F32 = jnp.float32
BF16 = jnp.bfloat16

D_MODEL = 1024
MLA_HEADS = 8
QK_NOPE = 64
QK_ROPE = 32
QK_DIM = QK_NOPE + QK_ROPE
V_HEAD = 64
Q_LORA = 384
KV_LORA = 256
ROPE_THETA = 10000.0
SSM_GROUP = 16
SSM_W = 256
SSM_GROUPS = SSM_W // SSM_GROUP
SSM_STATE = 64
MEM_HEADS = 4
MEM_HD = 64
MEM_W = MEM_HEADS * MEM_HD
N_BRANCH = 3
FF = -(-8 * D_MODEL // (3 * 256)) * 256
EPS = 1e-6
OFF_Q = Q_LORA
OFF_KV = OFF_Q + KV_LORA
OFF_KR = OFF_KV + QK_ROPE
OFF_SSM = OFF_KR + SSM_W
OFF_MEM = OFF_SSM + MEM_W

LANES = 128
HALF_ROPE = QK_ROPE // 2
ROPE_LO = QK_NOPE
ROPE_HI = QK_NOPE + HALF_ROPE
LOG2E = 1.4426950408889634
Q_SCALE = (QK_DIM ** -0.5) * LOG2E
MEM_Q_SCALE = (MEM_HD ** -0.5) * LOG2E
SAFE_SCORE_LOG2 = 60.0

Z_CQ = 0
Z_CKV = Z_CQ + Q_LORA
Z_KR = Z_CKV + KV_LORA
Z_KRS = Z_KR + LANES
Z_U = Z_KRS + LANES
Z_QM = Z_U + SSM_W
Z_END = Z_QM + MEM_HEADS * LANES

SSM_T = 32
SSM_TW = SSM_T * SSM_GROUP
SSM_SCAN_STEPS = 7

TOKEN_TILE = 512
ATTN_Q_TILE = 256
VMEM_LIMIT = 52 * 1024 * 1024


def _cparams(*sem):
    return pltpu.CompilerParams(dimension_semantics=sem, vmem_limit_bytes=VMEM_LIMIT)


def _const_spec(shape):
    n = len(shape)
    return pl.BlockSpec(shape, lambda *_: (0,) * n, pipeline_mode=pl.Buffered(1))


def _rms(x, g, n):
    ms = jnp.sum(x * x, axis=-1, keepdims=True) * (1.0 / n)
    return x * lax.rsqrt(ms + EPS) * g


def _sigmoid(x):
    return 1.0 / (1.0 + jnp.exp(-x))


def _dot(a, b):
    return jnp.dot(a, b, preferred_element_type=F32)


def _lane_onehot(lane_idx):
    return jnp.where(lax.broadcasted_iota(jnp.int32, (1, LANES), 1) == lane_idx, 1.0, 0.0)


def _head_rsqrt(t, n_valid, extra=None):
    ones = jnp.ones((LANES, LANES), BF16)
    ss = _dot((t * t).astype(BF16), ones)
    if extra is not None:
        ss = ss + extra
    return lax.rsqrt(ss * (1.0 / n_valid) + EPS), ss


def _mem_kv_kernel(mem_ref, g_ref, w_ref, gk_ref, mkt_ref, mv_ref):
    m = _rms(mem_ref[0], g_ref[...], D_MODEL).astype(BF16)
    kv = _dot(m, w_ref[...])
    for h in range(MEM_HEADS):
        k = _rms(kv[:, h * LANES:(h + 1) * LANES], gk_ref[...], MEM_HD)
        mkt_ref[0, h] = k.T.astype(BF16)
        off = (MEM_HEADS + h) * LANES
        mv_ref[0, h] = (kv[:, off:off + LANES] + _lane_onehot(MEM_HD)).astype(BF16)


def _mem_kv(mem, g_mem, w_kv, g_k):
    b, m, _ = mem.shape
    return pl.pallas_call(
        _mem_kv_kernel,
        grid=(b,),
        in_specs=[pl.BlockSpec((1, m, D_MODEL), lambda i: (i, 0, 0)),
                  _const_spec((1, D_MODEL)),
                  _const_spec((D_MODEL, 2 * MEM_HEADS * LANES)),
                  _const_spec((1, LANES))],
        out_specs=[pl.BlockSpec((1, MEM_HEADS, LANES, m), lambda i: (i, 0, 0, 0)),
                   pl.BlockSpec((1, MEM_HEADS, m, LANES), lambda i: (i, 0, 0, 0))],
        out_shape=[jax.ShapeDtypeStruct((b, MEM_HEADS, LANES, m), BF16),
                   jax.ShapeDtypeStruct((b, MEM_HEADS, m, LANES), BF16)],
        compiler_params=_cparams("arbitrary"),
        name="mem_kv",
    )(mem, g_mem, w_kv, g_k)


def _in_proj_kernel(x_ref, aq_ref, bq_ref, ak_ref, bk_ref, mkt_ref, mv_ref,
                    gmix_ref, w1_ref, gq_ref, wuq_ref, gkv_ref, wukv_ref, gmq_ref,
                    q_ref, kt_ref, v_ref, u_ref, om_ref):
    h = _rms(x_ref[0], gmix_ref[...], D_MODEL).astype(BF16)
    z = _dot(h, w1_ref[...])
    u_ref[0] = z[:, Z_U:Z_U + SSM_W]

    cq = _rms(z[:, Z_CQ:Z_CQ + Q_LORA], gq_ref[...], Q_LORA).astype(BF16)
    q2 = _dot(cq, wuq_ref[...])
    ckv = _rms(z[:, Z_CKV:Z_CKV + KV_LORA], gkv_ref[...], KV_LORA).astype(BF16)
    kv = _dot(ckv, wukv_ref[...])

    aq, bq, ak, bk = aq_ref[...], bq_ref[...], ak_ref[...], bk_ref[...]
    k_rope = z[:, Z_KR:Z_KR + LANES]
    k_rope_rot = k_rope * ak + z[:, Z_KRS:Z_KRS + LANES] * bk
    _, ss_rope = _head_rsqrt(k_rope, QK_DIM)
    hw = MLA_HEADS * LANES

    for hd in range(MLA_HEADS):
        sl = slice(hd * LANES, (hd + 1) * LANES)
        qh = q2[:, sl]
        r, _ = _head_rsqrt(qh, QK_DIM)
        q_ref[0, hd] = ((qh * aq + q2[:, hw + hd * LANES:hw + (hd + 1) * LANES] * bq) * r).astype(BF16)
        kh = kv[:, sl]
        r, _ = _head_rsqrt(kh, QK_DIM, extra=ss_rope)
        kt_ref[0, hd] = ((kh * ak + k_rope_rot) * r).T.astype(BF16)
        one = _lane_onehot(V_HEAD if hd % 2 == 0 else 0)
        v_ref[0, hd] = (kv[:, hw + hd * LANES:hw + (hd + 1) * LANES] + one).astype(BF16)

    for hd in range(MEM_HEADS):
        off = Z_QM + hd * LANES
        mq = z[:, off:off + LANES]
        r, _ = _head_rsqrt(mq, MEM_HD)
        s = _dot((mq * gmq_ref[...] * r).astype(BF16), mkt_ref[0, hd])
        p = jnp.exp2(s - jnp.max(s, axis=-1, keepdims=True))
        o = _dot(p.astype(BF16), mv_ref[0, hd])
        om_ref[0, :, hd * LANES:(hd + 1) * LANES] = (o * (1.0 / o[:, MEM_HD:MEM_HD + 1])).astype(BF16)


def _in_proj(x, rope, mkt, mv, wts):
    b, l, _ = x.shape
    tm = TOKEN_TILE
    m = mkt.shape[-1]
    tile = lambda last: pl.BlockSpec((1, tm, last), lambda i, j: (i, j, 0))
    head_tile = pl.BlockSpec((1, MLA_HEADS, tm, LANES), lambda i, j: (i, 0, j, 0))
    rope_spec = pl.BlockSpec((tm, LANES), lambda i, j: (j, 0))
    return pl.pallas_call(
        _in_proj_kernel,
        grid=(b, l // tm),
        in_specs=[tile(D_MODEL), rope_spec, rope_spec, rope_spec, rope_spec,
                  pl.BlockSpec((1, MEM_HEADS, LANES, m), lambda i, j: (i, 0, 0, 0)),
                  pl.BlockSpec((1, MEM_HEADS, m, LANES), lambda i, j: (i, 0, 0, 0))]
                 + [_const_spec(w.shape) for w in wts],
        out_specs=[head_tile,
                   pl.BlockSpec((1, MLA_HEADS, LANES, tm), lambda i, j: (i, 0, 0, j)),
                   head_tile, tile(SSM_W), tile(MEM_HEADS * LANES)],
        out_shape=[jax.ShapeDtypeStruct((b, MLA_HEADS, l, LANES), BF16),
                   jax.ShapeDtypeStruct((b, MLA_HEADS, LANES, l), BF16),
                   jax.ShapeDtypeStruct((b, MLA_HEADS, l, LANES), BF16),
                   jax.ShapeDtypeStruct((b, l, SSM_W), F32),
                   jax.ShapeDtypeStruct((b, l, MEM_HEADS * LANES), BF16)],
        compiler_params=_cparams("arbitrary", "arbitrary"),
        name="in_proj",
    )(x, *rope, mkt, mv, *wts)


def _attn_kernel(bound_ref, q_ref, kt_ref, v_ref, o_ref):
    tq = q_ref.shape[2]
    low_half = lax.broadcasted_iota(jnp.int32, (tq, LANES), 1) < V_HEAD

    def run(subtract_max):
        outs = []
        for j in range(2):
            s = _dot(q_ref[0, j], kt_ref[0, j])
            if subtract_max:
                s = s - jnp.max(s, axis=-1, keepdims=True)
            o = _dot(jnp.exp2(s).astype(BF16), v_ref[0, j])
            denom = o[:, V_HEAD:V_HEAD + 1] if j == 0 else o[:, 0:1]
            outs.append(o * (1.0 / denom))
        o_ref[0] = jnp.where(low_half, outs[0], outs[1]).astype(BF16)

    no_shift_needed = bound_ref[0] <= SAFE_SCORE_LOG2
    pl.when(no_shift_needed)(lambda: run(False))
    pl.when(jnp.logical_not(no_shift_needed))(lambda: run(True))


def _attention(score_bound, q, kt, v):
    b, _, l, _ = q.shape
    tq = ATTN_Q_TILE
    return pl.pallas_call(
        _attn_kernel,
        grid=(b, MLA_HEADS // 2, l // tq),
        in_specs=[pl.BlockSpec(memory_space=pltpu.SMEM),
                  pl.BlockSpec((1, 2, tq, LANES), lambda i, h, j: (i, h, j, 0)),
                  pl.BlockSpec((1, 2, LANES, l), lambda i, h, j: (i, h, 0, 0)),
                  pl.BlockSpec((1, 2, l, LANES), lambda i, h, j: (i, h, 0, 0))],
        out_specs=pl.BlockSpec((1, tq, LANES), lambda i, h, j: (i, j, h)),
        out_shape=jax.ShapeDtypeStruct((b, l, MLA_HEADS * V_HEAD), BF16),
        compiler_params=_cparams("arbitrary", "arbitrary", "arbitrary"),
        name="mla_attn",
    )(score_bound, q, kt, v)


def _cpow(r, th, e):
    mag = jnp.exp(r * e)
    return mag * jnp.cos(th * e), mag * jnp.sin(th * e)


def _zoh(lr, li, dt):
    a_re, a_im = _cpow(lr * dt, li * dt, 1.0)
    den = lr * lr + li * li
    n_re = a_re - 1.0
    return (n_re * lr + a_im * li) / den, (a_im * lr - n_re * li) / den


def _cmul(ar, ai, br, bi):
    return ar * br - ai * bi, ar * bi + ai * br


def _ssm_prep_kernel(prow_ref, pcol_ref, bt_ref, btt_ref, ctt_ref,
                     m_ref, wst_ref, cp_ref, ap_ref):
    t_row = (lax.broadcasted_iota(jnp.int32, (SSM_TW, SSM_STATE), 0) // SSM_GROUP).astype(F32)
    t_lane = (lax.broadcasted_iota(jnp.int32, (SSM_STATE, SSM_TW), 1) // SSM_GROUP).astype(F32)
    lane = lax.broadcasted_iota(jnp.int32, (SSM_GROUP, SSM_TW), 1)
    k_row = lax.broadcasted_iota(jnp.int32, (2 * SSM_SCAN_STEPS + 2, LANES), 0)
    chunk_pow = (SSM_T * jnp.left_shift(1, k_row // 2)).astype(F32)
    odd_row = (k_row % 2) == 1
    lane_ap = lax.broadcasted_iota(jnp.int32, (2 * SSM_SCAN_STEPS + 2, LANES), 1)

    toeplitz = []
    for d in range(2):
        lr2, li2 = prow_ref[0, 3 * d:3 * d + 1, :], prow_ref[0, 3 * d + 1:3 * d + 2, :]
        dt2 = jnp.exp(prow_ref[0, 3 * d + 2:3 * d + 3, :])
        lr, li, dt = lr2[:, :SSM_STATE], li2[:, :SSM_STATE], dt2[:, :SSM_STATE]
        z_re, z_im = _zoh(lr, li, dt)

        p_re, p_im = _cpow(lr2 * dt2, li2 * dt2, chunk_pow)
        ap_ref[0, d] = jnp.where(odd_row, jnp.where(lane_ap < SSM_STATE, -p_im, p_im), p_re)

        e_in = (SSM_T - 1.0) - t_row if d == 0 else t_row
        w_re, w_im = _cmul(*_cpow(lr * dt, li * dt, e_in), z_re, z_im)
        b_re, b_im = btt_ref[0, d, 0], btt_ref[0, d, 1]
        s_re, s_im = _cmul(w_re, w_im, b_re, b_im)
        wst_ref[0, :, 2 * d * SSM_STATE:(2 * d + 1) * SSM_STATE] = s_re.astype(BF16)
        wst_ref[0, :, (2 * d + 1) * SSM_STATE:(2 * d + 2) * SSM_STATE] = s_im.astype(BF16)

        lrc, lic = pcol_ref[0, :, 3 * d:3 * d + 1], pcol_ref[0, :, 3 * d + 1:3 * d + 2]
        dtc = jnp.exp(pcol_ref[0, :, 3 * d + 2:3 * d + 3])
        c_re, c_im = ctt_ref[0, d, 0], ctt_ref[0, d, 1]

        e_out = t_lane + 1.0 if d == 0 else SSM_T - t_lane
        g_re, g_im = _cmul(*_cpow(lrc * dtc, lic * dtc, e_out), c_re, c_im)
        cp_ref[0, 2 * d * SSM_STATE:(2 * d + 1) * SSM_STATE, :] = g_re.astype(BF16)
        cp_ref[0, (2 * d + 1) * SSM_STATE:(2 * d + 2) * SSM_STATE, :] = (-g_im).astype(BF16)

        e_lag = t_lane if d == 0 else (SSM_T - 1.0) - t_lane
        k_re, k_im = _cmul(*_cpow(lrc * dtc, lic * dtc, e_lag), c_re, c_im)
        bb_re, bb_im = _cmul(z_re, z_im, bt_ref[0, d, 0], bt_ref[0, d, 1])
        lag = (jnp.dot(bb_re, k_re, preferred_element_type=F32, precision=lax.Precision.HIGHEST)
               - jnp.dot(bb_im, k_im, preferred_element_type=F32, precision=lax.Precision.HIGHEST))
        toeplitz.append(lag)

    lag_f, lag_b = toeplitz
    for s in range(SSM_T):
        fwd = lag_f if s == 0 else pltpu.roll(lag_f, SSM_GROUP * s, 1)
        fwd = jnp.where(lane >= SSM_GROUP * s, fwd, 0.0)
        shift = (SSM_TW - SSM_GROUP * (SSM_T - 1 - s)) % SSM_TW
        bwd = lag_b if shift == 0 else pltpu.roll(lag_b, shift, 1)
        bwd = jnp.where(lane < SSM_GROUP * (s + 1), bwd, 0.0)
        m_ref[0, s * SSM_GROUP:(s + 1) * SSM_GROUP, :] = (fwd + bwd).astype(BF16)


def _ssm_prep(lam_re, lam_im, log_step, b_re, b_im, c_re, c_im):
    g, p, hg, t = SSM_GROUPS, SSM_STATE, SSM_GROUP, SSM_T
    step = jnp.broadcast_to(log_step[:, :, None], (2, g, p))
    rows = jnp.stack([lam_re[0], lam_im[0], step[0], lam_re[1], lam_im[1], step[1]], axis=1)
    rows = jnp.pad(rows, ((0, 0), (0, 2), (0, 0)))
    prow = jnp.concatenate([rows, rows], axis=-1)
    pcol = rows.transpose(0, 2, 1)
    bt = jnp.stack([b_re, b_im], axis=1).transpose(2, 0, 1, 4, 3)
    btt = jnp.tile(bt, (1, 1, 1, t, 1))
    ct = jnp.stack([c_re, c_im], axis=1).transpose(2, 0, 1, 4, 3)
    ctt = jnp.tile(ct, (1, 1, 1, 1, t))
    n_ap = 2 * SSM_SCAN_STEPS + 2
    blk = lambda *s: pl.BlockSpec((1,) + s, lambda i: (i,) + (0,) * len(s))
    return pl.pallas_call(
        _ssm_prep_kernel,
        grid=(g,),
        in_specs=[blk(8, LANES), blk(p, 8), blk(2, 2, hg, p), blk(2, 2, t * hg, p), blk(2, 2, p, t * hg)],
        out_specs=[blk(SSM_TW, SSM_TW), blk(SSM_TW, 4 * p), blk(4 * p, SSM_TW), blk(2, n_ap, LANES)],
        out_shape=[jax.ShapeDtypeStruct((g, SSM_TW, SSM_TW), BF16),
                   jax.ShapeDtypeStruct((g, SSM_TW, 4 * p), BF16),
                   jax.ShapeDtypeStruct((g, 4 * p, SSM_TW), BF16),
                   jax.ShapeDtypeStruct((g, 2, n_ap, LANES), F32)],
        compiler_params=_cparams("arbitrary"),
        name="ssm_prep",
    )(prow, pcol, bt, btt, ctt)


def _ssm_kernel(u_ref, m_ref, wst_ref, cp_ref, ap_ref, y_ref):
    u = u_ref[0, 0]
    nc = u.shape[0]
    z = _dot(u, wst_ref[0])
    row = lax.broadcasted_iota(jnp.int32, (nc, LANES), 0)
    sf, sb = z[:, :LANES], z[:, LANES:]
    for k in range(nc.bit_length() - 1):
        sh = 1 << k
        t = jnp.where(row >= sh, pltpu.roll(sf, sh, 0), 0.0)
        sf = sf + ap_ref[0, 0, 2 * k:2 * k + 1, :] * t \
            + ap_ref[0, 0, 2 * k + 1:2 * k + 2, :] * pltpu.roll(t, SSM_STATE, 1)
        t = jnp.where(row < nc - sh, pltpu.roll(sb, nc - sh, 0), 0.0)
        sb = sb + ap_ref[0, 1, 2 * k:2 * k + 1, :] * t \
            + ap_ref[0, 1, 2 * k + 1:2 * k + 2, :] * pltpu.roll(t, SSM_STATE, 1)
    s_prev = jnp.where(row >= 1, pltpu.roll(sf, 1, 0), 0.0)
    s_next = jnp.where(row < nc - 1, pltpu.roll(sb, nc - 1, 0), 0.0)
    s = jnp.concatenate([s_prev, s_next], axis=1).astype(BF16)
    y_ref[0, 0] = _dot(u, m_ref[0]) + _dot(s, cp_ref[0])


def _ssm(u_g, tables):
    m, wst, cp, ap = tables
    b, g, nc, tw = u_g.shape
    assert nc & (nc - 1) == 0 and nc <= 1 << SSM_SCAN_STEPS, "chunk count must be a power of two"
    tab = lambda a: pl.BlockSpec((1,) + a.shape[1:], lambda gi, bi: (gi,) + (0,) * (a.ndim - 1))
    io = pl.BlockSpec((1, 1, nc, tw), lambda gi, bi: (bi, gi, 0, 0))
    return pl.pallas_call(
        _ssm_kernel,
        grid=(g, b),
        in_specs=[io, tab(m), tab(wst), tab(cp), tab(ap)],
        out_specs=io,
        out_shape=jax.ShapeDtypeStruct((b, g, nc, tw), F32),
        compiler_params=_cparams("arbitrary", "arbitrary"),
        name="ssm_scan",
    )(u_g, m, wst, cp, ap)


def _gelu_tanh(x):
    return 0.5 * x * (1.0 + jnp.tanh(math.sqrt(2.0 / math.pi) * (x + 0.044715 * (x * x * x))))


def _merge_kernel(x_ref, oa_ref, y_ref, u_ref, om_ref,
                  gmix_ref, wg_ref, d_ref, wglu_ref, wa_ref, ws_ref, wm_ref, wout_ref, o_ref):
    x = x_ref[...]
    h = _rms(x, gmix_ref[...], D_MODEL).astype(BF16)
    y = _gelu_tanh(y_ref[...] + d_ref[...] * u_ref[...])
    o_ssm = (y * _sigmoid(_dot(y.astype(BF16), wglu_ref[...]))).astype(BF16)
    branches = ((oa_ref[...], wa_ref), (o_ssm, ws_ref), (om_ref[...], wm_ref))
    merged = None
    for i, (o_b, w_ref) in enumerate(branches):
        gate = _sigmoid(_dot(h, wg_ref[:, i * D_MODEL:(i + 1) * D_MODEL]))
        term = gate * _dot(o_b, w_ref[...])
        merged = term if merged is None else merged + term
    o_ref[...] = x + _dot(merged.astype(BF16), wout_ref[...])


def _merge(x2, oa2, y2, u2, om2, wts):
    n = x2.shape[0]
    tm = TOKEN_TILE
    tile = lambda a: pl.BlockSpec((tm, a.shape[1]), lambda i: (i, 0))
    return pl.pallas_call(
        _merge_kernel,
        grid=(n // tm,),
        in_specs=[tile(x2), tile(oa2), tile(y2), tile(u2), tile(om2)]
                 + [_const_spec(w.shape) for w in wts],
        out_specs=tile(x2),
        out_shape=jax.ShapeDtypeStruct(x2.shape, F32),
        compiler_params=_cparams("arbitrary"),
        name="merge_out",
    )(x2, oa2, y2, u2, om2, *wts)


def _ffn_kernel(x_ref, g_ref, wgate_ref, wup_ref, wdown_ref, o_ref):
    x = x_ref[...]
    h = _rms(x, g_ref[...], D_MODEL).astype(BF16)
    gate = _dot(h, wgate_ref[...])
    act = (gate * _sigmoid(gate) * _dot(h, wup_ref[...])).astype(BF16)
    o_ref[...] = x + _dot(act, wdown_ref[...])


def _ffn(x2, wts):
    n = x2.shape[0]
    tm = TOKEN_TILE
    tile = pl.BlockSpec((tm, D_MODEL), lambda i: (i, 0))
    return pl.pallas_call(
        _ffn_kernel,
        grid=(n // tm,),
        in_specs=[tile] + [_const_spec(w.shape) for w in wts],
        out_specs=tile,
        out_shape=jax.ShapeDtypeStruct(x2.shape, F32),
        compiler_params=_cparams("arbitrary"),
        name="ffn",
    )(x2, *wts)


def _pad_heads(w, heads, width):
    lead = w.shape[:-1]
    w = w.reshape(lead + (heads, width))
    w = jnp.pad(w, [(0, 0)] * len(lead) + [(0, 0), (0, LANES - width)])
    return w.reshape(lead + (heads * LANES,))


def _swap_rope_halves(w):
    lo, hi = w[..., ROPE_LO:ROPE_HI], w[..., ROPE_HI:QK_DIM]
    zeros = lambda n: jnp.zeros(w.shape[:-1] + (n,), w.dtype)
    return jnp.concatenate([zeros(QK_NOPE), hi, lo, zeros(LANES - QK_DIM)], axis=-1)


def _pad_row(g):
    return jnp.pad(g, (0, LANES - g.shape[0])).reshape(1, LANES)


def _rope_tables(length, gain, scale):
    inv = 1.0 / (ROPE_THETA ** (jnp.arange(0, QK_ROPE, 2, dtype=F32) / QK_ROPE))
    ang = jnp.arange(length, dtype=F32)[:, None] * inv[None, :]
    cos, sin = jnp.cos(ang), jnp.sin(ang)
    g = gain * scale
    g_nope, g_lo, g_hi = g[:QK_NOPE], g[ROPE_LO:ROPE_HI], g[ROPE_HI:QK_DIM]
    pad = jnp.zeros((length, LANES - QK_DIM), F32)
    a = jnp.concatenate([jnp.broadcast_to(g_nope, (length, QK_NOPE)), cos * g_lo, cos * g_hi, pad], axis=1)
    b = jnp.concatenate([jnp.zeros((length, QK_NOPE), F32), -sin * g_hi, sin * g_lo, pad], axis=1)
    return a, b


def _layer(x, mem, rope, score_bound, p):
    b, l, d = x.shape
    mkt, mv = _mem_kv(mem, p["g_mem"], p["w_mem_kv"], p["g_mk"])
    q, kt, v, u, o_mem = _in_proj(x, rope, mkt, mv, p["in_proj"])
    o_attn = _attention(score_bound, q, kt, v)

    nc = l // SSM_T
    u_g = u.astype(BF16).reshape(b, nc, SSM_T, SSM_GROUPS, SSM_GROUP)
    u_g = u_g.transpose(0, 3, 1, 2, 4).reshape(b, SSM_GROUPS, nc, SSM_TW)
    y_g = _ssm(u_g, p["ssm"])
    y = y_g.reshape(b, SSM_GROUPS, nc, SSM_T, SSM_GROUP).transpose(0, 2, 3, 1, 4).reshape(b * l, SSM_W)

    x2 = x.reshape(b * l, d)
    x1 = _merge(x2, o_attn.reshape(b * l, -1), y, u.reshape(b * l, SSM_W),
                o_mem.reshape(b * l, -1), p["merge"])
    return _ffn(x1, p["ffn"]).reshape(b, l, d)


def kernel(x_prompt, x_sample, mem_prompt, mem_sample, norm_mix_g, norm_mem_g, w_in, q_lora_norm_g, kv_lora_norm_g, w_uq, w_ukv, mla_q_norm_g, mla_k_norm_g, ssm_lambda_re, ssm_lambda_im, ssm_log_step, ssm_b_re, ssm_b_im, ssm_c_re, ssm_c_im, ssm_d, ssm_w_glu, w_mem_kv, mem_q_norm_g, mem_k_norm_g, w_br_attn, w_br_ssm, w_br_mem, w_out, norm_ffn_g, w_ffn_gate, w_ffn_up, w_ffn_down):
    assert norm_mix_g.shape[0] == 1, "single-layer encoder"
    row = lambda g: g.reshape(1, -1)
    w_in0 = w_in[0]

    k_rope_cols = jnp.pad(w_in0[:, OFF_KV:OFF_KR], ((0, 0), (QK_NOPE, LANES - QK_DIM)))
    w1 = jnp.concatenate([w_in0[:, :OFF_KV], k_rope_cols, _swap_rope_halves(k_rope_cols),
                          w_in0[:, OFF_KR:OFF_SSM],
                          _pad_heads(w_in0[:, OFF_SSM:OFF_MEM], MEM_HEADS, MEM_HD)], axis=1).astype(BF16)
    w_uq_p = _pad_heads(w_uq[0], MLA_HEADS, QK_DIM)
    w_uq_sw = _swap_rope_halves(w_uq_p.reshape(Q_LORA, MLA_HEADS, LANES)).reshape(Q_LORA, -1)
    w_uq2 = jnp.concatenate([w_uq_p, w_uq_sw], axis=1).astype(BF16)
    w_ukv_h = w_ukv[0].reshape(KV_LORA, MLA_HEADS, QK_NOPE + V_HEAD)
    w_k = _pad_heads(w_ukv_h[..., :QK_NOPE].reshape(KV_LORA, -1), MLA_HEADS, QK_NOPE)
    w_v = w_ukv_h[..., QK_NOPE:]
    w_v = jnp.concatenate([w_v, jnp.zeros_like(w_v)], axis=-1)
    w_v = jnp.where((jnp.arange(MLA_HEADS) % 2 == 1)[None, :, None], jnp.roll(w_v, V_HEAD, axis=-1), w_v)
    w_ukv_p = jnp.concatenate([w_k, w_v.reshape(KV_LORA, -1)], axis=1).astype(BF16)

    w_mkv = w_mem_kv[0]
    w_mkv_p = jnp.concatenate([_pad_heads(w_mkv[:, :MEM_W], MEM_HEADS, MEM_HD),
                               _pad_heads(w_mkv[:, MEM_W:], MEM_HEADS, MEM_HD)], axis=1).astype(BF16)
    w_br_mem_p = jnp.pad(w_br_mem[0].reshape(MEM_HEADS, MEM_HD, D_MODEL),
                         ((0, 0), (0, LANES - MEM_HD), (0, 0))).reshape(MEM_HEADS * LANES, D_MODEL)

    params = {
        "g_mem": row(norm_mem_g[0]), "w_mem_kv": w_mkv_p, "g_mk": _pad_row(mem_k_norm_g[0]),
        "in_proj": (row(norm_mix_g[0]), w1, row(q_lora_norm_g[0]), w_uq2, row(kv_lora_norm_g[0]), w_ukv_p,
                    _pad_row(mem_q_norm_g[0] * MEM_Q_SCALE)),
        "ssm": _ssm_prep(ssm_lambda_re[0], ssm_lambda_im[0], ssm_log_step[0],
                         ssm_b_re[0], ssm_b_im[0], ssm_c_re[0], ssm_c_im[0]),
        "merge": (row(norm_mix_g[0]), w_in0[:, OFF_MEM:].astype(BF16), row(ssm_d[0]),
                  ssm_w_glu[0].astype(BF16), w_br_attn[0].astype(BF16), w_br_ssm[0].astype(BF16),
                  w_br_mem_p.astype(BF16), w_out[0].astype(BF16)),
        "ffn": (row(norm_ffn_g[0]), w_ffn_gate[0].astype(BF16), w_ffn_up[0].astype(BF16),
                w_ffn_down[0].astype(BF16)),
    }
    score_bound = (1.02 * QK_DIM * Q_SCALE * jnp.max(jnp.abs(mla_q_norm_g[0]))
                   * jnp.max(jnp.abs(mla_k_norm_g[0]))).reshape(1).astype(F32)

    def rope(length):
        return (_rope_tables(length, mla_q_norm_g[0], Q_SCALE)
                + _rope_tables(length, mla_k_norm_g[0], 1.0))

    rope_p = rope(x_prompt.shape[1])
    rope_s = rope_p if x_sample.shape[1] == x_prompt.shape[1] else rope(x_sample.shape[1])
    y_prompt = _layer(x_prompt, mem_prompt, rope_p, score_bound, params)
    y_sample = _layer(x_sample, mem_sample, rope_s, score_bound, params)
    return (y_prompt, y_sample)
```

```python
import math

import jax
import jax.numpy as jnp
from jax import lax
from jax.experimental import pallas as pl
from jax.experimental.pallas import tpu as pltpu

F32 = jnp.float32
BF16 = jnp.bfloat16

D_MODEL = 1024
MLA_HEADS = 8
QK_NOPE = 64
QK_ROPE = 32
QK_DIM = QK_NOPE + QK_ROPE
V_HEAD = 64
Q_LORA = 384
KV_LORA = 256
ROPE_THETA = 10000.0
SSM_GROUP = 16
SSM_W = 256
SSM_GROUPS = SSM_W // SSM_GROUP
SSM_STATE = 64
MEM_HEADS = 4
MEM_HD = 64
MEM_W = MEM_HEADS * MEM_HD
N_BRANCH = 3
FF = -(-8 * D_MODEL // (3 * 256)) * 256
EPS = 1e-6
OFF_Q = Q_LORA
OFF_KV = OFF_Q + KV_LORA
OFF_KR = OFF_KV + QK_ROPE
OFF_SSM = OFF_KR + SSM_W
OFF_MEM = OFF_SSM + MEM_W

LANES = 128
HALF_ROPE = QK_ROPE // 2
ROPE_LO = QK_NOPE
ROPE_HI = QK_NOPE + HALF_ROPE
LOG2E = 1.4426950408889634
Q_SCALE = (QK_DIM ** -0.5) * LOG2E
MEM_Q_SCALE = (MEM_HD ** -0.5) * LOG2E
SAFE_SCORE_LOG2 = 60.0

Z_CQ = 0
Z_CKV = Z_CQ + Q_LORA
Z_KR = Z_CKV + KV_LORA
Z_KRS = Z_KR + LANES
Z_U = Z_KRS + LANES
Z_QM = Z_U + SSM_W
Z_END = Z_QM + MEM_HEADS * LANES

SSM_T = 32
SSM_TW = SSM_T * SSM_GROUP
SSM_SCAN_STEPS = 7
SSM_PER_TILE = LANES // SSM_GROUP
SSM_PERM_W = SSM_PER_TILE * LANES

TOKEN_TILE = 512
ATTN_Q_TILE = 256
VMEM_LIMIT = 52 * 1024 * 1024


def _cparams(*sem):
    return pltpu.CompilerParams(dimension_semantics=sem, vmem_limit_bytes=VMEM_LIMIT)


def _const_spec(shape):
    n = len(shape)
    return pl.BlockSpec(shape, lambda *_: (0,) * n, pipeline_mode=pl.Buffered(1))


def _rms(x, g, n):
    ms = jnp.sum(x * x, axis=-1, keepdims=True) * (1.0 / n)
    return x * lax.rsqrt(ms + EPS) * g


def _sigmoid(x):
    return 1.0 / (1.0 + jnp.exp(-x))


def _dot(a, b):
    return jnp.dot(a, b, preferred_element_type=F32)


def _lane_onehot(lane_idx):
    return jnp.where(lax.broadcasted_iota(jnp.int32, (1, LANES), 1) == lane_idx, 1.0, 0.0)


def _head_rsqrt(t, n_valid, extra=None):
    ones = jnp.ones((LANES, LANES), BF16)
    ss = _dot((t * t).astype(BF16), ones)
    if extra is not None:
        ss = ss + extra
    return lax.rsqrt(ss * (1.0 / n_valid) + EPS), ss


def _mem_kv_kernel(mem_ref, g_ref, w_ref, gk_ref, mkt_ref, mv_ref):
    m = _rms(mem_ref[0], g_ref[...], D_MODEL).astype(BF16)
    kv = _dot(m, w_ref[...])
    for h in range(MEM_HEADS):
        k = _rms(kv[:, h * LANES:(h + 1) * LANES], gk_ref[...], MEM_HD)
        mkt_ref[0, h] = k.T.astype(BF16)
        off = (MEM_HEADS + h) * LANES
        mv_ref[0, h] = (kv[:, off:off + LANES] + _lane_onehot(MEM_HD)).astype(BF16)


def _mem_kv(mem, g_mem, w_kv, g_k):
    b, m, _ = mem.shape
    return pl.pallas_call(
        _mem_kv_kernel,
        grid=(b,),
        in_specs=[pl.BlockSpec((1, m, D_MODEL), lambda i: (i, 0, 0)),
                  _const_spec((1, D_MODEL)),
                  _const_spec((D_MODEL, 2 * MEM_HEADS * LANES)),
                  _const_spec((1, LANES))],
        out_specs=[pl.BlockSpec((1, MEM_HEADS, LANES, m), lambda i: (i, 0, 0, 0)),
                   pl.BlockSpec((1, MEM_HEADS, m, LANES), lambda i: (i, 0, 0, 0))],
        out_shape=[jax.ShapeDtypeStruct((b, MEM_HEADS, LANES, m), BF16),
                   jax.ShapeDtypeStruct((b, MEM_HEADS, m, LANES), BF16)],
        compiler_params=_cparams("arbitrary"),
        name="mem_kv",
    )(mem, g_mem, w_kv, g_k)


def _in_proj_kernel(x_ref, aq_ref, bq_ref, ak_ref, bk_ref, mkt_ref, mv_ref,
                    gmix_ref, w1_ref, gq_ref, wuq_ref, gkv_ref, wukv_ref, gmq_ref,
                    q_ref, kt_ref, v_ref, u_ref, om_ref):
    h = _rms(x_ref[0], gmix_ref[...], D_MODEL).astype(BF16)
    z = _dot(h, w1_ref[...])
    u_ref[0] = z[:, Z_U:Z_U + SSM_W]

    cq = _rms(z[:, Z_CQ:Z_CQ + Q_LORA], gq_ref[...], Q_LORA).astype(BF16)
    q2 = _dot(cq, wuq_ref[...])
    ckv = _rms(z[:, Z_CKV:Z_CKV + KV_LORA], gkv_ref[...], KV_LORA).astype(BF16)
    kv = _dot(ckv, wukv_ref[...])

    aq, bq, ak, bk = aq_ref[...], bq_ref[...], ak_ref[...], bk_ref[...]
    k_rope = z[:, Z_KR:Z_KR + LANES]
    k_rope_rot = k_rope * ak + z[:, Z_KRS:Z_KRS + LANES] * bk
    _, ss_rope = _head_rsqrt(k_rope, QK_DIM)
    hw = MLA_HEADS * LANES

    for hd in range(MLA_HEADS):
        sl = slice(hd * LANES, (hd + 1) * LANES)
        qh = q2[:, sl]
        r, _ = _head_rsqrt(qh, QK_DIM)
        q_ref[0, hd] = ((qh * aq + q2[:, hw + hd * LANES:hw + (hd + 1) * LANES] * bq) * r).astype(BF16)
        kh = kv[:, sl]
        r, _ = _head_rsqrt(kh, QK_DIM, extra=ss_rope)
        kt_ref[0, hd] = ((kh * ak + k_rope_rot) * r).T.astype(BF16)
        one = _lane_onehot(V_HEAD if hd % 2 == 0 else 0)
        v_ref[0, hd] = (kv[:, hw + hd * LANES:hw + (hd + 1) * LANES] + one).astype(BF16)

    for hd in range(MEM_HEADS):
        off = Z_QM + hd * LANES
        mq = z[:, off:off + LANES]
        r, _ = _head_rsqrt(mq, MEM_HD)
        s = _dot((mq * gmq_ref[...] * r).astype(BF16), mkt_ref[0, hd])
        p = jnp.exp2(s - jnp.max(s, axis=-1, keepdims=True))
        o = _dot(p.astype(BF16), mv_ref[0, hd])
        om_ref[0, :, hd * LANES:(hd + 1) * LANES] = (o * (1.0 / o[:, MEM_HD:MEM_HD + 1])).astype(BF16)


def _in_proj(x, rope, mkt, mv, wts):
    b, l, _ = x.shape
    tm = TOKEN_TILE
    m = mkt.shape[-1]
    tile = lambda last: pl.BlockSpec((1, tm, last), lambda i, j: (i, j, 0))
    head_tile = pl.BlockSpec((1, MLA_HEADS, tm, LANES), lambda i, j: (i, 0, j, 0))
    rope_spec = pl.BlockSpec((tm, LANES), lambda i, j: (j, 0))
    return pl.pallas_call(
        _in_proj_kernel,
        grid=(b, l // tm),
        in_specs=[tile(D_MODEL), rope_spec, rope_spec, rope_spec, rope_spec,
                  pl.BlockSpec((1, MEM_HEADS, LANES, m), lambda i, j: (i, 0, 0, 0)),
                  pl.BlockSpec((1, MEM_HEADS, m, LANES), lambda i, j: (i, 0, 0, 0))]
                 + [_const_spec(w.shape) for w in wts],
        out_specs=[head_tile,
                   pl.BlockSpec((1, MLA_HEADS, LANES, tm), lambda i, j: (i, 0, 0, j)),
                   head_tile, tile(SSM_W), tile(MEM_HEADS * LANES)],
        out_shape=[jax.ShapeDtypeStruct((b, MLA_HEADS, l, LANES), BF16),
                   jax.ShapeDtypeStruct((b, MLA_HEADS, LANES, l), BF16),
                   jax.ShapeDtypeStruct((b, MLA_HEADS, l, LANES), BF16),
                   jax.ShapeDtypeStruct((b, l, SSM_W), F32),
                   jax.ShapeDtypeStruct((b, l, MEM_HEADS * LANES), BF16)],
        compiler_params=_cparams("arbitrary", "arbitrary"),
        name="in_proj",
    )(x, *rope, mkt, mv, *wts)


def _attn_kernel(bound_ref, q_ref, kt_ref, v_ref, o_ref):
    tq = q_ref.shape[2]
    low_half = lax.broadcasted_iota(jnp.int32, (tq, LANES), 1) < V_HEAD

    def run(subtract_max):
        outs = []
        for j in range(2):
            s = _dot(q_ref[0, j], kt_ref[0, j])
            if subtract_max:
                s = s - jnp.max(s, axis=-1, keepdims=True)
            o = _dot(jnp.exp2(s).astype(BF16), v_ref[0, j])
            denom = o[:, V_HEAD:V_HEAD + 1] if j == 0 else o[:, 0:1]
            outs.append(o * (1.0 / denom))
        o_ref[0] = jnp.where(low_half, outs[0], outs[1]).astype(BF16)

    no_shift_needed = bound_ref[0] <= SAFE_SCORE_LOG2
    pl.when(no_shift_needed)(lambda: run(False))
    pl.when(jnp.logical_not(no_shift_needed))(lambda: run(True))


def _attention(score_bound, q, kt, v):
    b, _, l, _ = q.shape
    tq = ATTN_Q_TILE
    return pl.pallas_call(
        _attn_kernel,
        grid=(b, MLA_HEADS // 2, l // tq),
        in_specs=[pl.BlockSpec(memory_space=pltpu.SMEM),
                  pl.BlockSpec((1, 2, tq, LANES), lambda i, h, j: (i, h, j, 0)),
                  pl.BlockSpec((1, 2, LANES, l), lambda i, h, j: (i, h, 0, 0)),
                  pl.BlockSpec((1, 2, l, LANES), lambda i, h, j: (i, h, 0, 0))],
        out_specs=pl.BlockSpec((1, tq, LANES), lambda i, h, j: (i, j, h)),
        out_shape=jax.ShapeDtypeStruct((b, l, MLA_HEADS * V_HEAD), BF16),
        compiler_params=_cparams("arbitrary", "arbitrary", "arbitrary"),
        name="mla_attn",
    )(score_bound, q, kt, v)


def _cpow(r, th, e):
    mag = jnp.exp(r * e)
    return mag * jnp.cos(th * e), mag * jnp.sin(th * e)


def _zoh(lr, li, dt):
    a_re, a_im = _cpow(lr * dt, li * dt, 1.0)
    den = lr * lr + li * li
    n_re = a_re - 1.0
    return (n_re * lr + a_im * li) / den, (a_im * lr - n_re * li) / den


def _cmul(ar, ai, br, bi):
    return ar * br - ai * bi, ar * bi + ai * br


def _ssm_prep_kernel(prow_ref, pcol_ref, bt_ref, btt_ref, ctt_ref,
                     m_ref, wst_ref, cp_ref, ap_ref):
    t_row = (lax.broadcasted_iota(jnp.int32, (SSM_TW, SSM_STATE), 0) // SSM_GROUP).astype(F32)
    t_lane = (lax.broadcasted_iota(jnp.int32, (SSM_STATE, SSM_TW), 1) // SSM_GROUP).astype(F32)
    lane = lax.broadcasted_iota(jnp.int32, (SSM_GROUP, SSM_TW), 1)
    k_row = lax.broadcasted_iota(jnp.int32, (2 * SSM_SCAN_STEPS + 2, LANES), 0)
    chunk_pow = (SSM_T * jnp.left_shift(1, k_row // 2)).astype(F32)
    odd_row = (k_row % 2) == 1
    lane_ap = lax.broadcasted_iota(jnp.int32, (2 * SSM_SCAN_STEPS + 2, LANES), 1)

    toeplitz = []
    for d in range(2):
        lr2, li2 = prow_ref[0, 3 * d:3 * d + 1, :], prow_ref[0, 3 * d + 1:3 * d + 2, :]
        dt2 = jnp.exp(prow_ref[0, 3 * d + 2:3 * d + 3, :])
        lr, li, dt = lr2[:, :SSM_STATE], li2[:, :SSM_STATE], dt2[:, :SSM_STATE]
        z_re, z_im = _zoh(lr, li, dt)

        p_re, p_im = _cpow(lr2 * dt2, li2 * dt2, chunk_pow)
        ap_ref[0, d] = jnp.where(odd_row, jnp.where(lane_ap < SSM_STATE, -p_im, p_im), p_re)

        e_in = (SSM_T - 1.0) - t_row if d == 0 else t_row
        w_re, w_im = _cmul(*_cpow(lr * dt, li * dt, e_in), z_re, z_im)
        b_re, b_im = btt_ref[0, d, 0], btt_ref[0, d, 1]
        s_re, s_im = _cmul(w_re, w_im, b_re, b_im)
        wst_ref[0, :, 2 * d * SSM_STATE:(2 * d + 1) * SSM_STATE] = s_re.astype(BF16)
        wst_ref[0, :, (2 * d + 1) * SSM_STATE:(2 * d + 2) * SSM_STATE] = s_im.astype(BF16)

        lrc, lic = pcol_ref[0, :, 3 * d:3 * d + 1], pcol_ref[0, :, 3 * d + 1:3 * d + 2]
        dtc = jnp.exp(pcol_ref[0, :, 3 * d + 2:3 * d + 3])
        c_re, c_im = ctt_ref[0, d, 0], ctt_ref[0, d, 1]

        e_out = t_lane + 1.0 if d == 0 else SSM_T - t_lane
        g_re, g_im = _cmul(*_cpow(lrc * dtc, lic * dtc, e_out), c_re, c_im)
        cp_ref[0, 2 * d * SSM_STATE:(2 * d + 1) * SSM_STATE, :] = g_re.astype(BF16)
        cp_ref[0, (2 * d + 1) * SSM_STATE:(2 * d + 2) * SSM_STATE, :] = (-g_im).astype(BF16)

        e_lag = t_lane if d == 0 else (SSM_T - 1.0) - t_lane
        k_re, k_im = _cmul(*_cpow(lrc * dtc, lic * dtc, e_lag), c_re, c_im)
        bb_re, bb_im = _cmul(z_re, z_im, bt_ref[0, d, 0], bt_ref[0, d, 1])
        lag = (jnp.dot(bb_re, k_re, preferred_element_type=F32, precision=lax.Precision.HIGHEST)
               - jnp.dot(bb_im, k_im, preferred_element_type=F32, precision=lax.Precision.HIGHEST))
        toeplitz.append(lag)

    lag_f, lag_b = toeplitz
    for s in range(SSM_T):
        fwd = lag_f if s == 0 else pltpu.roll(lag_f, SSM_GROUP * s, 1)
        fwd = jnp.where(lane >= SSM_GROUP * s, fwd, 0.0)
        shift = (SSM_TW - SSM_GROUP * (SSM_T - 1 - s)) % SSM_TW
        bwd = lag_b if shift == 0 else pltpu.roll(lag_b, shift, 1)
        bwd = jnp.where(lane < SSM_GROUP * (s + 1), bwd, 0.0)
        m_ref[0, s * SSM_GROUP:(s + 1) * SSM_GROUP, :] = (fwd + bwd).astype(BF16)


def _ssm_prep(lam_re, lam_im, log_step, b_re, b_im, c_re, c_im):
    g, p, hg, t = SSM_GROUPS, SSM_STATE, SSM_GROUP, SSM_T
    step = jnp.broadcast_to(log_step[:, :, None], (2, g, p))
    rows = jnp.stack([lam_re[0], lam_im[0], step[0], lam_re[1], lam_im[1], step[1]], axis=1)
    rows = jnp.pad(rows, ((0, 0), (0, 2), (0, 0)))
    prow = jnp.concatenate([rows, rows], axis=-1)
    pcol = rows.transpose(0, 2, 1)
    bt = jnp.stack([b_re, b_im], axis=1).transpose(2, 0, 1, 4, 3)
    btt = jnp.tile(bt, (1, 1, 1, t, 1))
    ct = jnp.stack([c_re, c_im], axis=1).transpose(2, 0, 1, 4, 3)
    ctt = jnp.tile(ct, (1, 1, 1, 1, t))
    n_ap = 2 * SSM_SCAN_STEPS + 2
    blk = lambda *s: pl.BlockSpec((1,) + s, lambda i: (i,) + (0,) * len(s))
    return pl.pallas_call(
        _ssm_prep_kernel,
        grid=(g,),
        in_specs=[blk(8, LANES), blk(p, 8), blk(2, 2, hg, p), blk(2, 2, t * hg, p), blk(2, 2, p, t * hg)],
        out_specs=[blk(SSM_TW, SSM_TW), blk(SSM_TW, 4 * p), blk(4 * p, SSM_TW), blk(2, n_ap, LANES)],
        out_shape=[jax.ShapeDtypeStruct((g, SSM_TW, SSM_TW), BF16),
                   jax.ShapeDtypeStruct((g, SSM_TW, 4 * p), BF16),
                   jax.ShapeDtypeStruct((g, 4 * p, SSM_TW), BF16),
                   jax.ShapeDtypeStruct((g, 2, n_ap, LANES), F32)],
        compiler_params=_cparams("arbitrary"),
        name="ssm_prep",
    )(prow, pcol, bt, btt, ctt)


def _atom_transpose_perm():
    i = jnp.arange(SSM_PERM_W)
    a, b, h = i // LANES, (i % LANES) // SSM_GROUP, i % SSM_GROUP
    j = b * LANES + a * SSM_GROUP + h
    return (j[:, None] == i[None, :]).astype(BF16)


def _ssm_kernel(u_lo_ref, u_hi_ref, perm_ref, m_ref, wst_ref, cp_ref, ap_ref, y_lo_ref, y_hi_ref,
                ug_ref, yg_ref, z_ref, s_ref):
    u_refs, y_refs = (u_lo_ref, u_hi_ref), (y_lo_ref, y_hi_ref)
    nc = ug_ref.shape[1]
    perm = perm_ref[...]
    blocks = [(tb, half) for tb in range(SSM_T // SSM_PER_TILE) for half in range(SSM_GROUPS // SSM_PER_TILE)]

    def token_rows(refs, tb, t, half):
        return refs[half].at[0, pl.ds(tb * SSM_PER_TILE + t, nc, stride=SSM_T), :]

    for tb, half in blocks:
        x = jnp.concatenate([token_rows(u_refs, tb, t, half)[...] for t in range(SSM_PER_TILE)], axis=1)
        r = _dot(x.astype(BF16), perm).astype(BF16)
        for gi in range(SSM_PER_TILE):
            ug_ref[half * SSM_PER_TILE + gi, :, tb * LANES:(tb + 1) * LANES] = r[:, gi * LANES:(gi + 1) * LANES]

    def state_in(g, carry):
        z_ref[g] = _dot(ug_ref[g], wst_ref[g])
        return carry

    lax.fori_loop(0, SSM_GROUPS, state_in, 0)

    rows = SSM_GROUPS * nc
    chunk = lax.broadcasted_iota(jnp.int32, (rows, LANES), 0) & (nc - 1)

    def shifted(s3, sh, d):
        flat = s3.reshape(rows, LANES)
        if d == 0:
            return jnp.where(chunk >= sh, pltpu.roll(flat, sh, 0), 0.0)
        return jnp.where(chunk < nc - sh, pltpu.roll(flat, rows - sh, 0), 0.0)

    as3d = lambda a: a.reshape(SSM_GROUPS, nc, LANES)
    for d in range(2):
        s3 = z_ref[:, :, d * LANES:(d + 1) * LANES]
        for k in range(nc.bit_length() - 1):
            t = shifted(s3, 1 << k, d)
            s3 = s3 + ap_ref[:, d, 2 * k:2 * k + 1, :] * as3d(t) \
                + ap_ref[:, d, 2 * k + 1:2 * k + 2, :] * as3d(pltpu.roll(t, SSM_STATE, 1))
        s_ref[:, :, d * LANES:(d + 1) * LANES] = as3d(shifted(s3, 1, d)).astype(BF16)

    def chunk_out(g, carry):
        yg_ref[g] = _dot(ug_ref[g], m_ref[g]) + _dot(s_ref[g], cp_ref[g])
        return carry

    lax.fori_loop(0, SSM_GROUPS, chunk_out, 0)

    for tb, half in blocks:
        yv = jnp.concatenate([yg_ref[half * SSM_PER_TILE + gi, :, tb * LANES:(tb + 1) * LANES]
                              for gi in range(SSM_PER_TILE)], axis=1)
        hi = yv.astype(BF16)
        lo = (yv - hi.astype(F32)).astype(BF16)
        out = _dot(hi, perm) + _dot(lo, perm)
        for t in range(SSM_PER_TILE):
            token_rows(y_refs, tb, t, half)[...] = out[:, t * LANES:(t + 1) * LANES]


def _ssm(u, tables):
    m, wst, cp, ap = tables
    b, l, w = u.shape
    nc = l // SSM_T
    assert nc & (nc - 1) == 0 and nc <= 1 << SSM_SCAN_STEPS, "chunk count must be a power of two"
    perm = _atom_transpose_perm()
    assert w == 2 * LANES
    half = lambda j: pl.BlockSpec((1, l, LANES), lambda i: (i, 0, j))
    return pl.pallas_call(
        _ssm_kernel,
        grid=(b,),
        in_specs=[half(0), half(1)] + [_const_spec(a.shape) for a in (perm, m, wst, cp, ap)],
        out_specs=[half(0), half(0)],
        out_shape=[jax.ShapeDtypeStruct((b, l, LANES), F32)] * 2,
        scratch_shapes=[pltpu.VMEM((SSM_GROUPS, nc, SSM_TW), BF16),
                        pltpu.VMEM((SSM_GROUPS, nc, SSM_TW), F32),
                        pltpu.VMEM((SSM_GROUPS, nc, 2 * LANES), F32),
                        pltpu.VMEM((SSM_GROUPS, nc, 2 * LANES), BF16)],
        compiler_params=_cparams("arbitrary"),
        name="ssm_scan",
    )(u, u, perm, m, wst, cp, ap)


def _gelu_tanh(x):
    return 0.5 * x * (1.0 + jnp.tanh(math.sqrt(2.0 / math.pi) * (x + 0.044715 * (x * x * x))))


def _merge_kernel(x_ref, oa_ref, y_lo_ref, y_hi_ref, u_ref, om_ref,
                  gmix_ref, wg_ref, d_ref, wglu_ref, wa_ref, ws_ref, wm_ref, wout_ref, o_ref):
    x = x_ref[...]
    h = _rms(x, gmix_ref[...], D_MODEL).astype(BF16)
    y = jnp.concatenate([y_lo_ref[...], y_hi_ref[...]], axis=1)
    y = _gelu_tanh(y + d_ref[...] * u_ref[...])
    o_ssm = (y * _sigmoid(_dot(y.astype(BF16), wglu_ref[...]))).astype(BF16)
    branches = ((oa_ref[...], wa_ref), (o_ssm, ws_ref), (om_ref[...], wm_ref))
    merged = None
    for i, (o_b, w_ref) in enumerate(branches):
        gate = _sigmoid(_dot(h, wg_ref[:, i * D_MODEL:(i + 1) * D_MODEL]))
        term = gate * _dot(o_b, w_ref[...])
        merged = term if merged is None else merged + term
    o_ref[...] = x + _dot(merged.astype(BF16), wout_ref[...])


def _merge(x2, oa2, y_lo, y_hi, u2, om2, wts):
    n = x2.shape[0]
    tm = TOKEN_TILE
    tile = lambda a: pl.BlockSpec((tm, a.shape[1]), lambda i: (i, 0))
    return pl.pallas_call(
        _merge_kernel,
        grid=(n // tm,),
        in_specs=[tile(x2), tile(oa2), tile(y_lo), tile(y_hi), tile(u2), tile(om2)]
                 + [_const_spec(w.shape) for w in wts],
        out_specs=tile(x2),
        out_shape=jax.ShapeDtypeStruct(x2.shape, F32),
        compiler_params=_cparams("arbitrary"),
        name="merge_out",
    )(x2, oa2, y_lo, y_hi, u2, om2, *wts)


def _ffn_kernel(x_ref, g_ref, wgate_ref, wup_ref, wdown_ref, o_ref):
    x = x_ref[...]
    h = _rms(x, g_ref[...], D_MODEL).astype(BF16)
    gate = _dot(h, wgate_ref[...])
    act = (gate * _sigmoid(gate) * _dot(h, wup_ref[...])).astype(BF16)
    o_ref[...] = x + _dot(act, wdown_ref[...])


def _ffn(x2, wts):
    n = x2.shape[0]
    tm = TOKEN_TILE
    tile = pl.BlockSpec((tm, D_MODEL), lambda i: (i, 0))
    return pl.pallas_call(
        _ffn_kernel,
        grid=(n // tm,),
        in_specs=[tile] + [_const_spec(w.shape) for w in wts],
        out_specs=tile,
        out_shape=jax.ShapeDtypeStruct(x2.shape, F32),
        compiler_params=_cparams("arbitrary"),
        name="ffn",
    )(x2, *wts)


def _pad_heads(w, heads, width):
    lead = w.shape[:-1]
    w = w.reshape(lead + (heads, width))
    w = jnp.pad(w, [(0, 0)] * len(lead) + [(0, 0), (0, LANES - width)])
    return w.reshape(lead + (heads * LANES,))


def _swap_rope_halves(w):
    lo, hi = w[..., ROPE_LO:ROPE_HI], w[..., ROPE_HI:QK_DIM]
    zeros = lambda n: jnp.zeros(w.shape[:-1] + (n,), w.dtype)
    return jnp.concatenate([zeros(QK_NOPE), hi, lo, zeros(LANES - QK_DIM)], axis=-1)


def _pad_row(g):
    return jnp.pad(g, (0, LANES - g.shape[0])).reshape(1, LANES)


def _rope_tables(length, gain, scale):
    inv = 1.0 / (ROPE_THETA ** (jnp.arange(0, QK_ROPE, 2, dtype=F32) / QK_ROPE))
    ang = jnp.arange(length, dtype=F32)[:, None] * inv[None, :]
    cos, sin = jnp.cos(ang), jnp.sin(ang)
    g = gain * scale
    g_nope, g_lo, g_hi = g[:QK_NOPE], g[ROPE_LO:ROPE_HI], g[ROPE_HI:QK_DIM]
    pad = jnp.zeros((length, LANES - QK_DIM), F32)
    a = jnp.concatenate([jnp.broadcast_to(g_nope, (length, QK_NOPE)), cos * g_lo, cos * g_hi, pad], axis=1)
    b = jnp.concatenate([jnp.zeros((length, QK_NOPE), F32), -sin * g_hi, sin * g_lo, pad], axis=1)
    return a, b


def _layer(x, mem, rope, score_bound, p):
    b, l, d = x.shape
    mkt, mv = _mem_kv(mem, p["g_mem"], p["w_mem_kv"], p["g_mk"])
    q, kt, v, u, o_mem = _in_proj(x, rope, mkt, mv, p["in_proj"])
    o_attn = _attention(score_bound, q, kt, v)

    y_lo, y_hi = (y.reshape(b * l, LANES) for y in _ssm(u, p["ssm"]))

    x2 = x.reshape(b * l, d)
    x1 = _merge(x2, o_attn.reshape(b * l, -1), y_lo, y_hi, u.reshape(b * l, SSM_W),
                o_mem.reshape(b * l, -1), p["merge"])
    return _ffn(x1, p["ffn"]).reshape(b, l, d)


def kernel(x_prompt, x_sample, mem_prompt, mem_sample, norm_mix_g, norm_mem_g, w_in, q_lora_norm_g, kv_lora_norm_g, w_uq, w_ukv, mla_q_norm_g, mla_k_norm_g, ssm_lambda_re, ssm_lambda_im, ssm_log_step, ssm_b_re, ssm_b_im, ssm_c_re, ssm_c_im, ssm_d, ssm_w_glu, w_mem_kv, mem_q_norm_g, mem_k_norm_g, w_br_attn, w_br_ssm, w_br_mem, w_out, norm_ffn_g, w_ffn_gate, w_ffn_up, w_ffn_down):
    assert norm_mix_g.shape[0] == 1, "single-layer encoder"
    row = lambda g: g.reshape(1, -1)
    w_in0 = w_in[0]

    k_rope_cols = jnp.pad(w_in0[:, OFF_KV:OFF_KR], ((0, 0), (QK_NOPE, LANES - QK_DIM)))
    w1 = jnp.concatenate([w_in0[:, :OFF_KV], k_rope_cols, _swap_rope_halves(k_rope_cols),
                          w_in0[:, OFF_KR:OFF_SSM],
                          _pad_heads(w_in0[:, OFF_SSM:OFF_MEM], MEM_HEADS, MEM_HD)], axis=1).astype(BF16)
    w_uq_p = _pad_heads(w_uq[0], MLA_HEADS, QK_DIM)
    w_uq_sw = _swap_rope_halves(w_uq_p.reshape(Q_LORA, MLA_HEADS, LANES)).reshape(Q_LORA, -1)
    w_uq2 = jnp.concatenate([w_uq_p, w_uq_sw], axis=1).astype(BF16)
    w_ukv_h = w_ukv[0].reshape(KV_LORA, MLA_HEADS, QK_NOPE + V_HEAD)
    w_k = _pad_heads(w_ukv_h[..., :QK_NOPE].reshape(KV_LORA, -1), MLA_HEADS, QK_NOPE)
    w_v = w_ukv_h[..., QK_NOPE:]
    w_v = jnp.concatenate([w_v, jnp.zeros_like(w_v)], axis=-1)
    w_v = jnp.where((jnp.arange(MLA_HEADS) % 2 == 1)[None, :, None], jnp.roll(w_v, V_HEAD, axis=-1), w_v)
    w_ukv_p = jnp.concatenate([w_k, w_v.reshape(KV_LORA, -1)], axis=1).astype(BF16)

    w_mkv = w_mem_kv[0]
    w_mkv_p = jnp.concatenate([_pad_heads(w_mkv[:, :MEM_W], MEM_HEADS, MEM_HD),
                               _pad_heads(w_mkv[:, MEM_W:], MEM_HEADS, MEM_HD)], axis=1).astype(BF16)
    w_br_mem_p = jnp.pad(w_br_mem[0].reshape(MEM_HEADS, MEM_HD, D_MODEL),
                         ((0, 0), (0, LANES - MEM_HD), (0, 0))).reshape(MEM_HEADS * LANES, D_MODEL)

    params = {
        "g_mem": row(norm_mem_g[0]), "w_mem_kv": w_mkv_p, "g_mk": _pad_row(mem_k_norm_g[0]),
        "in_proj": (row(norm_mix_g[0]), w1, row(q_lora_norm_g[0]), w_uq2, row(kv_lora_norm_g[0]), w_ukv_p,
                    _pad_row(mem_q_norm_g[0] * MEM_Q_SCALE)),
        "ssm": _ssm_prep(ssm_lambda_re[0], ssm_lambda_im[0], ssm_log_step[0],
                         ssm_b_re[0], ssm_b_im[0], ssm_c_re[0], ssm_c_im[0]),
        "merge": (row(norm_mix_g[0]), w_in0[:, OFF_MEM:].astype(BF16), row(ssm_d[0]),
                  ssm_w_glu[0].astype(BF16), w_br_attn[0].astype(BF16), w_br_ssm[0].astype(BF16),
                  w_br_mem_p.astype(BF16), w_out[0].astype(BF16)),
        "ffn": (row(norm_ffn_g[0]), w_ffn_gate[0].astype(BF16), w_ffn_up[0].astype(BF16),
                w_ffn_down[0].astype(BF16)),
    }
    score_bound = (1.02 * QK_DIM * Q_SCALE * jnp.max(jnp.abs(mla_q_norm_g[0]))
                   * jnp.max(jnp.abs(mla_k_norm_g[0]))).reshape(1).astype(F32)

    def rope(length):
        return (_rope_tables(length, mla_q_norm_g[0], Q_SCALE)
                + _rope_tables(length, mla_k_norm_g[0], 1.0))

    rope_p = rope(x_prompt.shape[1])
    rope_s = rope_p if x_sample.shape[1] == x_prompt.shape[1] else rope(x_sample.shape[1])
    y_prompt = _layer(x_prompt, mem_prompt, rope_p, score_bound, params)
    y_sample = _layer(x_sample, mem_sample, rope_s, score_bound, params)
    return (y_prompt, y_sample)
```

```python
import math

import jax
import jax.numpy as jnp
from jax import lax
from jax.experimental import pallas as pl
from jax.experimental.pallas import tpu as pltpu

F32 = jnp.float32
BF16 = jnp.bfloat16

D_MODEL = 1024
MLA_HEADS = 8
QK_NOPE = 64
QK_ROPE = 32
QK_DIM = QK_NOPE + QK_ROPE
V_HEAD = 64
Q_LORA = 384
KV_LORA = 256
ROPE_THETA = 10000.0
SSM_GROUP = 16
SSM_W = 256
SSM_GROUPS = SSM_W // SSM_GROUP
SSM_STATE = 64
MEM_HEADS = 4
MEM_HD = 64
MEM_W = MEM_HEADS * MEM_HD
N_BRANCH = 3
FF = -(-8 * D_MODEL // (3 * 256)) * 256
EPS = 1e-6
OFF_Q = Q_LORA
OFF_KV = OFF_Q + KV_LORA
OFF_KR = OFF_KV + QK_ROPE
OFF_SSM = OFF_KR + SSM_W
OFF_MEM = OFF_SSM + MEM_W

LANES = 128
HALF_ROPE = QK_ROPE // 2
ROPE_LO = QK_NOPE
ROPE_HI = QK_NOPE + HALF_ROPE
LOG2E = 1.4426950408889634
Q_SCALE = (QK_DIM ** -0.5) * LOG2E
MEM_Q_SCALE = (MEM_HD ** -0.5) * LOG2E
SAFE_SCORE_LOG2 = 60.0

Z_CQ = 0
Z_CKV = Z_CQ + Q_LORA
Z_KR = Z_CKV + KV_LORA
Z_KRS = Z_KR + LANES
Z_U = Z_KRS + LANES
Z_QM = Z_U + SSM_W
Z_END = Z_QM + MEM_HEADS * LANES

SSM_T = 32
SSM_TW = SSM_T * SSM_GROUP
SSM_SCAN_STEPS = 7
SSM_PER_TILE = LANES // SSM_GROUP
SSM_PERM_W = SSM_PER_TILE * LANES

VT_ROWS = V_HEAD + 16

TOKEN_TILE = 512
ATTN_Q_TILE = 512
VMEM_LIMIT = 52 * 1024 * 1024


def _cparams(*sem):
    return pltpu.CompilerParams(dimension_semantics=sem, vmem_limit_bytes=VMEM_LIMIT)


def _const_spec(shape):
    n = len(shape)
    return pl.BlockSpec(shape, lambda *_: (0,) * n, pipeline_mode=pl.Buffered(1))


def _rms(x, g, n):
    ms = jnp.sum(x * x, axis=-1, keepdims=True) * (1.0 / n)
    return x * lax.rsqrt(ms + EPS) * g


def _sigmoid(x):
    return 1.0 / (1.0 + jnp.exp(-x))


def _dot(a, b):
    return jnp.dot(a, b, preferred_element_type=F32)


def _lane_onehot(lane_idx):
    return jnp.where(lax.broadcasted_iota(jnp.int32, (1, LANES), 1) == lane_idx, 1.0, 0.0)


def _head_rsqrt(t, n_valid, extra=None):
    ones = jnp.ones((LANES, LANES), BF16)
    ss = _dot((t * t).astype(BF16), ones)
    if extra is not None:
        ss = ss + extra
    return lax.rsqrt(ss * (1.0 / n_valid) + EPS), ss


def _mem_kv_kernel(mem_ref, g_ref, w_ref, gk_ref, mkt_ref, mv_ref):
    m = _rms(mem_ref[0], g_ref[...], D_MODEL).astype(BF16)
    kv = _dot(m, w_ref[...])
    for h in range(MEM_HEADS):
        k = _rms(kv[:, h * LANES:(h + 1) * LANES], gk_ref[...], MEM_HD)
        mkt_ref[0, h] = k.T.astype(BF16)
        off = (MEM_HEADS + h) * LANES
        mv_ref[0, h] = (kv[:, off:off + LANES] + _lane_onehot(MEM_HD)).astype(BF16)


def _mem_kv(mem, g_mem, w_kv, g_k):
    b, m, _ = mem.shape
    return pl.pallas_call(
        _mem_kv_kernel,
        grid=(b,),
        in_specs=[pl.BlockSpec((1, m, D_MODEL), lambda i: (i, 0, 0)),
                  _const_spec((1, D_MODEL)),
                  _const_spec((D_MODEL, 2 * MEM_HEADS * LANES)),
                  _const_spec((1, LANES))],
        out_specs=[pl.BlockSpec((1, MEM_HEADS, LANES, m), lambda i: (i, 0, 0, 0)),
                   pl.BlockSpec((1, MEM_HEADS, m, LANES), lambda i: (i, 0, 0, 0))],
        out_shape=[jax.ShapeDtypeStruct((b, MEM_HEADS, LANES, m), BF16),
                   jax.ShapeDtypeStruct((b, MEM_HEADS, m, LANES), BF16)],
        compiler_params=_cparams("arbitrary"),
        name="mem_kv",
    )(mem, g_mem, w_kv, g_k)


def _in_proj_kernel(x_ref, aq_ref, bq_ref, ak_ref, bk_ref, mkt_ref, mv_ref,
                    gmix_ref, w1_ref, gq_ref, wuq_ref, gkv_ref, wukv_ref, gmq_ref,
                    q_ref, k_ref, vt_ref, u_ref, om_ref):
    h = _rms(x_ref[0], gmix_ref[...], D_MODEL).astype(BF16)
    z = _dot(h, w1_ref[...])
    u_ref[0] = z[:, Z_U:Z_U + SSM_W]

    cq = _rms(z[:, Z_CQ:Z_CQ + Q_LORA], gq_ref[...], Q_LORA).astype(BF16)
    q2 = _dot(cq, wuq_ref[...])
    ckv = _rms(z[:, Z_CKV:Z_CKV + KV_LORA], gkv_ref[...], KV_LORA).astype(BF16)
    kv = _dot(ckv, wukv_ref[...])

    aq, bq, ak, bk = aq_ref[...], bq_ref[...], ak_ref[...], bk_ref[...]
    k_rope = z[:, Z_KR:Z_KR + LANES]
    k_rope_rot = k_rope * ak + z[:, Z_KRS:Z_KRS + LANES] * bk
    _, ss_rope = _head_rsqrt(k_rope, QK_DIM)
    hw = MLA_HEADS * LANES

    for hd in range(MLA_HEADS):
        sl = slice(hd * LANES, (hd + 1) * LANES)
        qh = q2[:, sl]
        r, _ = _head_rsqrt(qh, QK_DIM)
        q_ref[0, hd] = ((qh * aq + q2[:, hw + hd * LANES:hw + (hd + 1) * LANES] * bq) * r).astype(BF16)
        kh = kv[:, sl]
        r, _ = _head_rsqrt(kh, QK_DIM, extra=ss_rope)
        k_ref[0, hd] = ((kh * ak + k_rope_rot) * r).astype(BF16)

    v_t = kv[:, hw:hw + MLA_HEADS * V_HEAD].T.astype(BF16)
    tm = v_t.shape[1]
    ones_rows = jnp.where(lax.broadcasted_iota(jnp.int32, (VT_ROWS - V_HEAD, tm), 0) == 0, 1.0, 0.0)
    for hd in range(MLA_HEADS):
        vt_ref[0, hd, :V_HEAD, :] = v_t[hd * V_HEAD:(hd + 1) * V_HEAD, :]
        vt_ref[0, hd, V_HEAD:, :] = ones_rows.astype(BF16)

    for hd in range(MEM_HEADS):
        off = Z_QM + hd * LANES
        mq = z[:, off:off + LANES]
        r, _ = _head_rsqrt(mq, MEM_HD)
        s = _dot((mq * gmq_ref[...] * r).astype(BF16), mkt_ref[0, hd])
        p = jnp.exp2(s - jnp.max(s, axis=-1, keepdims=True))
        o = _dot(p.astype(BF16), mv_ref[0, hd])
        om_ref[0, :, hd * LANES:(hd + 1) * LANES] = (o * (1.0 / o[:, MEM_HD:MEM_HD + 1])).astype(BF16)


def _in_proj(x, rope, mkt, mv, wts):
    b, l, _ = x.shape
    tm = TOKEN_TILE
    m = mkt.shape[-1]
    tile = lambda last: pl.BlockSpec((1, tm, last), lambda i, j: (i, j, 0))
    head_tile = pl.BlockSpec((1, MLA_HEADS, tm, LANES), lambda i, j: (i, 0, j, 0))
    rope_spec = pl.BlockSpec((tm, LANES), lambda i, j: (j, 0))
    return pl.pallas_call(
        _in_proj_kernel,
        grid=(b, l // tm),
        in_specs=[tile(D_MODEL), rope_spec, rope_spec, rope_spec, rope_spec,
                  pl.BlockSpec((1, MEM_HEADS, LANES, m), lambda i, j: (i, 0, 0, 0)),
                  pl.BlockSpec((1, MEM_HEADS, m, LANES), lambda i, j: (i, 0, 0, 0))]
                 + [_const_spec(w.shape) for w in wts],
        out_specs=[head_tile, head_tile,
                   pl.BlockSpec((1, MLA_HEADS, VT_ROWS, tm), lambda i, j: (i, 0, 0, j)),
                   tile(SSM_W), tile(MEM_HEADS * LANES)],
        out_shape=[jax.ShapeDtypeStruct((b, MLA_HEADS, l, LANES), BF16),
                   jax.ShapeDtypeStruct((b, MLA_HEADS, l, LANES), BF16),
                   jax.ShapeDtypeStruct((b, MLA_HEADS, VT_ROWS, l), BF16),
                   jax.ShapeDtypeStruct((b, l, SSM_W), F32),
                   jax.ShapeDtypeStruct((b, l, MEM_HEADS * LANES), BF16)],
        compiler_params=_cparams("arbitrary", "arbitrary"),
        name="in_proj",
    )(x, *rope, mkt, mv, *wts)


def _attn_kernel(bound_ref, q_ref, k_ref, vt_ref, o_ref):
    def run(subtract_max):
        scores = []
        for j in range(2):
            s_t = lax.dot_general(k_ref[0, j], q_ref[0, j], (((1,), (1,)), ((), ())),
                                  preferred_element_type=F32)
            if subtract_max:
                s_t = s_t - jnp.max(s_t, axis=0, keepdims=True)
            scores.append(s_t)
        outs = []
        for j in range(2):
            o_t = _dot(vt_ref[0, j], jnp.exp2(scores[j]).astype(BF16))
            outs.append(o_t[:V_HEAD] * (1.0 / o_t[V_HEAD:V_HEAD + 1]))
        o_ref[0] = jnp.concatenate(outs, axis=0).T.astype(BF16)

    no_shift_needed = bound_ref[0] <= SAFE_SCORE_LOG2
    pl.when(no_shift_needed)(lambda: run(False))
    pl.when(jnp.logical_not(no_shift_needed))(lambda: run(True))


def _attention(score_bound, q, k, vt):
    b, _, l, _ = q.shape
    tq = ATTN_Q_TILE
    return pl.pallas_call(
        _attn_kernel,
        grid=(b, MLA_HEADS // 2, l // tq),
        in_specs=[pl.BlockSpec(memory_space=pltpu.SMEM),
                  pl.BlockSpec((1, 2, tq, LANES), lambda i, h, j: (i, h, j, 0)),
                  pl.BlockSpec((1, 2, l, LANES), lambda i, h, j: (i, h, 0, 0)),
                  pl.BlockSpec((1, 2, VT_ROWS, l), lambda i, h, j: (i, h, 0, 0))],
        out_specs=pl.BlockSpec((1, tq, LANES), lambda i, h, j: (i, j, h)),
        out_shape=jax.ShapeDtypeStruct((b, l, MLA_HEADS * V_HEAD), BF16),
        compiler_params=_cparams("arbitrary", "arbitrary", "arbitrary"),
        name="mla_attn",
    )(score_bound, q, k, vt)


def _cpow(r, th, e):
    mag = jnp.exp(r * e)
    return mag * jnp.cos(th * e), mag * jnp.sin(th * e)


def _zoh(lr, li, dt):
    a_re, a_im = _cpow(lr * dt, li * dt, 1.0)
    den = lr * lr + li * li
    n_re = a_re - 1.0
    return (n_re * lr + a_im * li) / den, (a_im * lr - n_re * li) / den


def _cmul(ar, ai, br, bi):
    return ar * br - ai * bi, ar * bi + ai * br


def _ssm_prep_kernel(prow_ref, pcol_ref, bt_ref, btt_ref, ctt_ref,
                     m_ref, wst_ref, cp_ref, ap_ref):
    t_row = (lax.broadcasted_iota(jnp.int32, (SSM_TW, SSM_STATE), 0) // SSM_GROUP).astype(F32)
    t_lane = (lax.broadcasted_iota(jnp.int32, (SSM_STATE, SSM_TW), 1) // SSM_GROUP).astype(F32)
    lane = lax.broadcasted_iota(jnp.int32, (SSM_GROUP, SSM_TW), 1)
    k_row = lax.broadcasted_iota(jnp.int32, (2 * SSM_SCAN_STEPS + 2, LANES), 0)
    chunk_pow = (SSM_T * jnp.left_shift(1, k_row // 2)).astype(F32)
    odd_row = (k_row % 2) == 1
    lane_ap = lax.broadcasted_iota(jnp.int32, (2 * SSM_SCAN_STEPS + 2, LANES), 1)

    toeplitz = []
    for d in range(2):
        lr2, li2 = prow_ref[0, 3 * d:3 * d + 1, :], prow_ref[0, 3 * d + 1:3 * d + 2, :]
        dt2 = jnp.exp(prow_ref[0, 3 * d + 2:3 * d + 3, :])
        lr, li, dt = lr2[:, :SSM_STATE], li2[:, :SSM_STATE], dt2[:, :SSM_STATE]
        z_re, z_im = _zoh(lr, li, dt)

        p_re, p_im = _cpow(lr2 * dt2, li2 * dt2, chunk_pow)
        ap_ref[0, d] = jnp.where(odd_row, jnp.where(lane_ap < SSM_STATE, -p_im, p_im), p_re)

        e_in = (SSM_T - 1.0) - t_row if d == 0 else t_row
        w_re, w_im = _cmul(*_cpow(lr * dt, li * dt, e_in), z_re, z_im)
        b_re, b_im = btt_ref[0, d, 0], btt_ref[0, d, 1]
        s_re, s_im = _cmul(w_re, w_im, b_re, b_im)
        wst_ref[0, :, 2 * d * SSM_STATE:(2 * d + 1) * SSM_STATE] = s_re.astype(BF16)
        wst_ref[0, :, (2 * d + 1) * SSM_STATE:(2 * d + 2) * SSM_STATE] = s_im.astype(BF16)

        lrc, lic = pcol_ref[0, :, 3 * d:3 * d + 1], pcol_ref[0, :, 3 * d + 1:3 * d + 2]
        dtc = jnp.exp(pcol_ref[0, :, 3 * d + 2:3 * d + 3])
        c_re, c_im = ctt_ref[0, d, 0], ctt_ref[0, d, 1]

        e_out = t_lane + 1.0 if d == 0 else SSM_T - t_lane
        g_re, g_im = _cmul(*_cpow(lrc * dtc, lic * dtc, e_out), c_re, c_im)
        cp_ref[0, 2 * d * SSM_STATE:(2 * d + 1) * SSM_STATE, :] = g_re.astype(BF16)
        cp_ref[0, (2 * d + 1) * SSM_STATE:(2 * d + 2) * SSM_STATE, :] = (-g_im).astype(BF16)

        e_lag = t_lane if d == 0 else (SSM_T - 1.0) - t_lane
        k_re, k_im = _cmul(*_cpow(lrc * dtc, lic * dtc, e_lag), c_re, c_im)
        bb_re, bb_im = _cmul(z_re, z_im, bt_ref[0, d, 0], bt_ref[0, d, 1])
        lag = (jnp.dot(bb_re, k_re, preferred_element_type=F32, precision=lax.Precision.HIGHEST)
               - jnp.dot(bb_im, k_im, preferred_element_type=F32, precision=lax.Precision.HIGHEST))
        toeplitz.append(lag)

    lag_f, lag_b = toeplitz
    for s in range(SSM_T):
        fwd = lag_f if s == 0 else pltpu.roll(lag_f, SSM_GROUP * s, 1)
        fwd = jnp.where(lane >= SSM_GROUP * s, fwd, 0.0)
        shift = (SSM_TW - SSM_GROUP * (SSM_T - 1 - s)) % SSM_TW
        bwd = lag_b if shift == 0 else pltpu.roll(lag_b, shift, 1)
        bwd = jnp.where(lane < SSM_GROUP * (s + 1), bwd, 0.0)
        m_ref[0, s * SSM_GROUP:(s + 1) * SSM_GROUP, :] = (fwd + bwd).astype(BF16)


def _ssm_prep(lam_re, lam_im, log_step, b_re, b_im, c_re, c_im):
    g, p, hg, t = SSM_GROUPS, SSM_STATE, SSM_GROUP, SSM_T
    step = jnp.broadcast_to(log_step[:, :, None], (2, g, p))
    rows = jnp.stack([lam_re[0], lam_im[0], step[0], lam_re[1], lam_im[1], step[1]], axis=1)
    rows = jnp.pad(rows, ((0, 0), (0, 2), (0, 0)))
    prow = jnp.concatenate([rows, rows], axis=-1)
    pcol = rows.transpose(0, 2, 1)
    bt = jnp.stack([b_re, b_im], axis=1).transpose(2, 0, 1, 4, 3)
    btt = jnp.tile(bt, (1, 1, 1, t, 1))
    ct = jnp.stack([c_re, c_im], axis=1).transpose(2, 0, 1, 4, 3)
    ctt = jnp.tile(ct, (1, 1, 1, 1, t))
    n_ap = 2 * SSM_SCAN_STEPS + 2
    blk = lambda *s: pl.BlockSpec((1,) + s, lambda i: (i,) + (0,) * len(s))
    return pl.pallas_call(
        _ssm_prep_kernel,
        grid=(g,),
        in_specs=[blk(8, LANES), blk(p, 8), blk(2, 2, hg, p), blk(2, 2, t * hg, p), blk(2, 2, p, t * hg)],
        out_specs=[blk(SSM_TW, SSM_TW), blk(SSM_TW, 4 * p), blk(4 * p, SSM_TW), blk(2, n_ap, LANES)],
        out_shape=[jax.ShapeDtypeStruct((g, SSM_TW, SSM_TW), BF16),
                   jax.ShapeDtypeStruct((g, SSM_TW, 4 * p), BF16),
                   jax.ShapeDtypeStruct((g, 4 * p, SSM_TW), BF16),
                   jax.ShapeDtypeStruct((g, 2, n_ap, LANES), F32)],
        compiler_params=_cparams("arbitrary"),
        name="ssm_prep",
    )(prow, pcol, bt, btt, ctt)


def _atom_transpose_perm():
    i = jnp.arange(SSM_PERM_W)
    a, b, h = i // LANES, (i % LANES) // SSM_GROUP, i % SSM_GROUP
    j = b * LANES + a * SSM_GROUP + h
    return (j[:, None] == i[None, :]).astype(BF16)


def _ssm_kernel(u_lo_ref, u_hi_ref, perm_ref, m_ref, wst_ref, cp_ref, ap_ref, y_lo_ref, y_hi_ref,
                ug_ref, yg_ref, z_ref, s_ref):
    u_refs, y_refs = (u_lo_ref, u_hi_ref), (y_lo_ref, y_hi_ref)
    nc = ug_ref.shape[1]
    perm = perm_ref[...]
    blocks = [(tb, half) for tb in range(SSM_T // SSM_PER_TILE) for half in range(SSM_GROUPS // SSM_PER_TILE)]

    def token_rows(refs, tb, t, half):
        return refs[half].at[0, pl.ds(tb * SSM_PER_TILE + t, nc, stride=SSM_T), :]

    for tb, half in blocks:
        x = jnp.concatenate([token_rows(u_refs, tb, t, half)[...] for t in range(SSM_PER_TILE)], axis=1)
        r = _dot(x.astype(BF16), perm).astype(BF16)
        for gi in range(SSM_PER_TILE):
            ug_ref[half * SSM_PER_TILE + gi, :, tb * LANES:(tb + 1) * LANES] = r[:, gi * LANES:(gi + 1) * LANES]

    def state_in(g, carry):
        z_ref[g] = _dot(ug_ref[g], wst_ref[g])
        return carry

    lax.fori_loop(0, SSM_GROUPS, state_in, 0)

    rows = SSM_GROUPS * nc
    chunk = lax.broadcasted_iota(jnp.int32, (rows, LANES), 0) & (nc - 1)

    def shifted(s3, sh, d):
        flat = s3.reshape(rows, LANES)
        if d == 0:
            return jnp.where(chunk >= sh, pltpu.roll(flat, sh, 0), 0.0)
        return jnp.where(chunk < nc - sh, pltpu.roll(flat, rows - sh, 0), 0.0)

    as3d = lambda a: a.reshape(SSM_GROUPS, nc, LANES)
    for d in range(2):
        s3 = z_ref[:, :, d * LANES:(d + 1) * LANES]
        for k in range(nc.bit_length() - 1):
            t = shifted(s3, 1 << k, d)
            s3 = s3 + ap_ref[:, d, 2 * k:2 * k + 1, :] * as3d(t) \
                + ap_ref[:, d, 2 * k + 1:2 * k + 2, :] * as3d(pltpu.roll(t, SSM_STATE, 1))
        s_ref[:, :, d * LANES:(d + 1) * LANES] = as3d(shifted(s3, 1, d)).astype(BF16)

    def chunk_out(g, carry):
        yg_ref[g] = _dot(ug_ref[g], m_ref[g]) + _dot(s_ref[g], cp_ref[g])
        return carry

    lax.fori_loop(0, SSM_GROUPS, chunk_out, 0)

    for tb, half in blocks:
        yv = jnp.concatenate([yg_ref[half * SSM_PER_TILE + gi, :, tb * LANES:(tb + 1) * LANES]
                              for gi in range(SSM_PER_TILE)], axis=1)
        hi = yv.astype(BF16)
        lo = (yv - hi.astype(F32)).astype(BF16)
        out = _dot(hi, perm) + _dot(lo, perm)
        for t in range(SSM_PER_TILE):
            token_rows(y_refs, tb, t, half)[...] = out[:, t * LANES:(t + 1) * LANES]


def _ssm(u, tables):
    m, wst, cp, ap = tables
    b, l, w = u.shape
    nc = l // SSM_T
    assert nc & (nc - 1) == 0 and nc <= 1 << SSM_SCAN_STEPS, "chunk count must be a power of two"
    perm = _atom_transpose_perm()
    assert w == 2 * LANES
    half = lambda j: pl.BlockSpec((1, l, LANES), lambda i: (i, 0, j))
    return pl.pallas_call(
        _ssm_kernel,
        grid=(b,),
        in_specs=[half(0), half(1)] + [_const_spec(a.shape) for a in (perm, m, wst, cp, ap)],
        out_specs=[half(0), half(0)],
        out_shape=[jax.ShapeDtypeStruct((b, l, LANES), F32)] * 2,
        scratch_shapes=[pltpu.VMEM((SSM_GROUPS, nc, SSM_TW), BF16),
                        pltpu.VMEM((SSM_GROUPS, nc, SSM_TW), F32),
                        pltpu.VMEM((SSM_GROUPS, nc, 2 * LANES), F32),
                        pltpu.VMEM((SSM_GROUPS, nc, 2 * LANES), BF16)],
        compiler_params=_cparams("arbitrary"),
        name="ssm_scan",
    )(u, u, perm, m, wst, cp, ap)


def _gelu_tanh(x):
    return 0.5 * x * (1.0 + jnp.tanh(math.sqrt(2.0 / math.pi) * (x + 0.044715 * (x * x * x))))


def _merge_kernel(x_ref, oa_ref, y_lo_ref, y_hi_ref, u_ref, om_ref,
                  gmix_ref, wg_ref, d_ref, wglu_ref, wa_ref, ws_ref, wm_ref, wout_ref, o_ref):
    x = x_ref[...]
    h = _rms(x, gmix_ref[...], D_MODEL).astype(BF16)
    y = jnp.concatenate([y_lo_ref[...], y_hi_ref[...]], axis=1)
    y = _gelu_tanh(y + d_ref[...] * u_ref[...])
    o_ssm = (y * _sigmoid(_dot(y.astype(BF16), wglu_ref[...]))).astype(BF16)
    branches = ((oa_ref[...], wa_ref), (o_ssm, ws_ref), (om_ref[...], wm_ref))
    merged = None
    for i, (o_b, w_ref) in enumerate(branches):
        gate = _sigmoid(_dot(h, wg_ref[:, i * D_MODEL:(i + 1) * D_MODEL]))
        term = gate * _dot(o_b, w_ref[...])
        merged = term if merged is None else merged + term
    o_ref[...] = x + _dot(merged.astype(BF16), wout_ref[...])


def _merge(x2, oa2, y_lo, y_hi, u2, om2, wts):
    n = x2.shape[0]
    tm = TOKEN_TILE
    tile = lambda a: pl.BlockSpec((tm, a.shape[1]), lambda i: (i, 0))
    return pl.pallas_call(
        _merge_kernel,
        grid=(n // tm,),
        in_specs=[tile(x2), tile(oa2), tile(y_lo), tile(y_hi), tile(u2), tile(om2)]
                 + [_const_spec(w.shape) for w in wts],
        out_specs=tile(x2),
        out_shape=jax.ShapeDtypeStruct(x2.shape, F32),
        compiler_params=_cparams("arbitrary"),
        name="merge_out",
    )(x2, oa2, y_lo, y_hi, u2, om2, *wts)


def _ffn_kernel(x_ref, g_ref, wgate_ref, wup_ref, wdown_ref, o_ref):
    x = x_ref[...]
    h = _rms(x, g_ref[...], D_MODEL).astype(BF16)
    gate = _dot(h, wgate_ref[...])
    act = (gate * _sigmoid(gate) * _dot(h, wup_ref[...])).astype(BF16)
    o_ref[...] = x + _dot(act, wdown_ref[...])


def _ffn(x2, wts):
    n = x2.shape[0]
    tm = TOKEN_TILE
    tile = pl.BlockSpec((tm, D_MODEL), lambda i: (i, 0))
    return pl.pallas_call(
        _ffn_kernel,
        grid=(n // tm,),
        in_specs=[tile] + [_const_spec(w.shape) for w in wts],
        out_specs=tile,
        out_shape=jax.ShapeDtypeStruct(x2.shape, F32),
        compiler_params=_cparams("arbitrary"),
        name="ffn",
    )(x2, *wts)


def _pad_heads(w, heads, width):
    lead = w.shape[:-1]
    w = w.reshape(lead + (heads, width))
    w = jnp.pad(w, [(0, 0)] * len(lead) + [(0, 0), (0, LANES - width)])
    return w.reshape(lead + (heads * LANES,))


def _swap_rope_halves(w):
    lo, hi = w[..., ROPE_LO:ROPE_HI], w[..., ROPE_HI:QK_DIM]
    zeros = lambda n: jnp.zeros(w.shape[:-1] + (n,), w.dtype)
    return jnp.concatenate([zeros(QK_NOPE), hi, lo, zeros(LANES - QK_DIM)], axis=-1)


def _pad_row(g):
    return jnp.pad(g, (0, LANES - g.shape[0])).reshape(1, LANES)


def _rope_tables(length, gain, scale):
    inv = 1.0 / (ROPE_THETA ** (jnp.arange(0, QK_ROPE, 2, dtype=F32) / QK_ROPE))
    ang = jnp.arange(length, dtype=F32)[:, None] * inv[None, :]
    cos, sin = jnp.cos(ang), jnp.sin(ang)
    g = gain * scale
    g_nope, g_lo, g_hi = g[:QK_NOPE], g[ROPE_LO:ROPE_HI], g[ROPE_HI:QK_DIM]
    pad = jnp.zeros((length, LANES - QK_DIM), F32)
    a = jnp.concatenate([jnp.broadcast_to(g_nope, (length, QK_NOPE)), cos * g_lo, cos * g_hi, pad], axis=1)
    b = jnp.concatenate([jnp.zeros((length, QK_NOPE), F32), -sin * g_hi, sin * g_lo, pad], axis=1)
    return a, b


def _layer(x, mem, rope, score_bound, p):
    b, l, d = x.shape
    mkt, mv = _mem_kv(mem, p["g_mem"], p["w_mem_kv"], p["g_mk"])
    q, k, vt, u, o_mem = _in_proj(x, rope, mkt, mv, p["in_proj"])
    o_attn = _attention(score_bound, q, k, vt)

    y_lo, y_hi = (y.reshape(b * l, LANES) for y in _ssm(u, p["ssm"]))

    x2 = x.reshape(b * l, d)
    x1 = _merge(x2, o_attn.reshape(b * l, -1), y_lo, y_hi, u.reshape(b * l, SSM_W),
                o_mem.reshape(b * l, -1), p["merge"])
    return _ffn(x1, p["ffn"]).reshape(b, l, d)


def kernel(x_prompt, x_sample, mem_prompt, mem_sample, norm_mix_g, norm_mem_g, w_in, q_lora_norm_g, kv_lora_norm_g, w_uq, w_ukv, mla_q_norm_g, mla_k_norm_g, ssm_lambda_re, ssm_lambda_im, ssm_log_step, ssm_b_re, ssm_b_im, ssm_c_re, ssm_c_im, ssm_d, ssm_w_glu, w_mem_kv, mem_q_norm_g, mem_k_norm_g, w_br_attn, w_br_ssm, w_br_mem, w_out, norm_ffn_g, w_ffn_gate, w_ffn_up, w_ffn_down):
    assert norm_mix_g.shape[0] == 1, "single-layer encoder"
    row = lambda g: g.reshape(1, -1)
    w_in0 = w_in[0]

    k_rope_cols = jnp.pad(w_in0[:, OFF_KV:OFF_KR], ((0, 0), (QK_NOPE, LANES - QK_DIM)))
    w1 = jnp.concatenate([w_in0[:, :OFF_KV], k_rope_cols, _swap_rope_halves(k_rope_cols),
                          w_in0[:, OFF_KR:OFF_SSM],
                          _pad_heads(w_in0[:, OFF_SSM:OFF_MEM], MEM_HEADS, MEM_HD)], axis=1).astype(BF16)
    w_uq_p = _pad_heads(w_uq[0], MLA_HEADS, QK_DIM)
    w_uq_sw = _swap_rope_halves(w_uq_p.reshape(Q_LORA, MLA_HEADS, LANES)).reshape(Q_LORA, -1)
    w_uq2 = jnp.concatenate([w_uq_p, w_uq_sw], axis=1).astype(BF16)
    w_ukv_h = w_ukv[0].reshape(KV_LORA, MLA_HEADS, QK_NOPE + V_HEAD)
    w_k = _pad_heads(w_ukv_h[..., :QK_NOPE].reshape(KV_LORA, -1), MLA_HEADS, QK_NOPE)
    w_v = w_ukv_h[..., QK_NOPE:].reshape(KV_LORA, -1)
    w_ukv_p = jnp.concatenate([w_k, w_v], axis=1).astype(BF16)

    w_mkv = w_mem_kv[0]
    w_mkv_p = jnp.concatenate([_pad_heads(w_mkv[:, :MEM_W], MEM_HEADS, MEM_HD),
                               _pad_heads(w_mkv[:, MEM_W:], MEM_HEADS, MEM_HD)], axis=1).astype(BF16)
    w_br_mem_p = jnp.pad(w_br_mem[0].reshape(MEM_HEADS, MEM_HD, D_MODEL),
                         ((0, 0), (0, LANES - MEM_HD), (0, 0))).reshape(MEM_HEADS * LANES, D_MODEL)

    params = {
        "g_mem": row(norm_mem_g[0]), "w_mem_kv": w_mkv_p, "g_mk": _pad_row(mem_k_norm_g[0]),
        "in_proj": (row(norm_mix_g[0]), w1, row(q_lora_norm_g[0]), w_uq2, row(kv_lora_norm_g[0]), w_ukv_p,
                    _pad_row(mem_q_norm_g[0] * MEM_Q_SCALE)),
        "ssm": _ssm_prep(ssm_lambda_re[0], ssm_lambda_im[0], ssm_log_step[0],
                         ssm_b_re[0], ssm_b_im[0], ssm_c_re[0], ssm_c_im[0]),
        "merge": (row(norm_mix_g[0]), w_in0[:, OFF_MEM:].astype(BF16), row(ssm_d[0]),
                  ssm_w_glu[0].astype(BF16), w_br_attn[0].astype(BF16), w_br_ssm[0].astype(BF16),
                  w_br_mem_p.astype(BF16), w_out[0].astype(BF16)),
        "ffn": (row(norm_ffn_g[0]), w_ffn_gate[0].astype(BF16), w_ffn_up[0].astype(BF16),
                w_ffn_down[0].astype(BF16)),
    }
    score_bound = (1.02 * QK_DIM * Q_SCALE * jnp.max(jnp.abs(mla_q_norm_g[0]))
                   * jnp.max(jnp.abs(mla_k_norm_g[0]))).reshape(1).astype(F32)

    def rope(length):
        return (_rope_tables(length, mla_q_norm_g[0], Q_SCALE)
                + _rope_tables(length, mla_k_norm_g[0], 1.0))

    rope_p = rope(x_prompt.shape[1])
    rope_s = rope_p if x_sample.shape[1] == x_prompt.shape[1] else rope(x_sample.shape[1])
    y_prompt = _layer(x_prompt, mem_prompt, rope_p, score_bound, params)
    y_sample = _layer(x_sample, mem_sample, rope_s, score_bound, params)
    return (y_prompt, y_sample)
```

```python
import math

import jax
import jax.numpy as jnp
from jax import lax
from jax.experimental import pallas as pl
from jax.experimental.pallas import tpu as pltpu

F32 = jnp.float32
BF16 = jnp.bfloat16

D_MODEL = 1024
MLA_HEADS = 8
QK_NOPE = 64
QK_ROPE = 32
QK_DIM = QK_NOPE + QK_ROPE
V_HEAD = 64
Q_LORA = 384
KV_LORA = 256
ROPE_THETA = 10000.0
SSM_GROUP = 16
SSM_W = 256
SSM_GROUPS = SSM_W // SSM_GROUP
SSM_STATE = 64
MEM_HEADS = 4
MEM_HD = 64
MEM_W = MEM_HEADS * MEM_HD
N_BRANCH = 3
FF = -(-8 * D_MODEL // (3 * 256)) * 256
EPS = 1e-6
OFF_Q = Q_LORA
OFF_KV = OFF_Q + KV_LORA
OFF_KR = OFF_KV + QK_ROPE
OFF_SSM = OFF_KR + SSM_W
OFF_MEM = OFF_SSM + MEM_W

LANES = 128
HALF_ROPE = QK_ROPE // 2
ROPE_LO = QK_NOPE
ROPE_HI = QK_NOPE + HALF_ROPE
LOG2E = 1.4426950408889634
Q_SCALE = (QK_DIM ** -0.5) * LOG2E
MEM_Q_SCALE = (MEM_HD ** -0.5) * LOG2E
SAFE_SCORE_LOG2 = 60.0

Z_CQ = 0
Z_CKV = Z_CQ + Q_LORA
Z_KR = Z_CKV + KV_LORA
Z_U = Z_KR + LANES
Z_QM = Z_U + SSM_W
Z_END = Z_QM + MEM_HEADS * LANES

SSM_T = 32
SSM_TW = SSM_T * SSM_GROUP
SSM_SCAN_STEPS = 7
SSM_PER_TILE = LANES // SSM_GROUP
SSM_PERM_W = SSM_PER_TILE * LANES

VT_ROWS = V_HEAD + 16

TOKEN_TILE = 512
IN_PROJ_TILE = 1024
ATTN_Q_TILE = 512
VMEM_LIMIT = 52 * 1024 * 1024


def _cparams(*sem):
    return pltpu.CompilerParams(dimension_semantics=sem, vmem_limit_bytes=VMEM_LIMIT)


def _const_spec(shape):
    n = len(shape)
    return pl.BlockSpec(shape, lambda *_: (0,) * n, pipeline_mode=pl.Buffered(1))


def _rms(x, g, n):
    ms = jnp.sum(x * x, axis=-1, keepdims=True) * (1.0 / n)
    return x * lax.rsqrt(ms + EPS) * g


def _sigmoid(x):
    return 1.0 / (1.0 + jnp.exp(-x))


def _dot(a, b):
    return jnp.dot(a, b, preferred_element_type=F32)


def _lane_onehot(lane_idx):
    return jnp.where(lax.broadcasted_iota(jnp.int32, (1, LANES), 1) == lane_idx, 1.0, 0.0)


def _mem_ones_lane(head):
    return MEM_HD if head % 2 == 0 else 0


def _head_rsqrt_mxu(t, n_valid):
    ones = jnp.where(lax.broadcasted_iota(jnp.int32, (LANES, LANES), 0) < n_valid, 1.0, 0.0)
    ss = _dot((t * t).astype(BF16), ones.astype(BF16))
    return lax.rsqrt(ss * (1.0 / n_valid) + EPS)


def _swap_rope(t):
    return pltpu.roll(t, LANES - HALF_ROPE, 1)


def _mem_kv_kernel(mem_ref, g_ref, w_ref, gk_ref, mkt_ref, mv_ref):
    m = _rms(mem_ref[0], g_ref[...], D_MODEL).astype(BF16)
    kv = _dot(m, w_ref[...])
    for h in range(MEM_HEADS):
        k = _rms(kv[:, h * LANES:(h + 1) * LANES], gk_ref[...], MEM_HD)
        mkt_ref[0, h] = k.T.astype(BF16)
        off = (MEM_HEADS + h) * LANES
        mv_ref[0, h] = (kv[:, off:off + LANES] + _lane_onehot(_mem_ones_lane(h))).astype(BF16)


def _mem_kv(mem, g_mem, w_kv, g_k):
    b, m, _ = mem.shape
    return pl.pallas_call(
        _mem_kv_kernel,
        grid=(b,),
        in_specs=[pl.BlockSpec((1, m, D_MODEL), lambda i: (i, 0, 0)),
                  _const_spec((1, D_MODEL)),
                  _const_spec((D_MODEL, 2 * MEM_HEADS * LANES)),
                  _const_spec((1, LANES))],
        out_specs=[pl.BlockSpec((1, MEM_HEADS, LANES, m), lambda i: (i, 0, 0, 0)),
                   pl.BlockSpec((1, MEM_HEADS, m, LANES), lambda i: (i, 0, 0, 0))],
        out_shape=[jax.ShapeDtypeStruct((b, MEM_HEADS, LANES, m), BF16),
                   jax.ShapeDtypeStruct((b, MEM_HEADS, m, LANES), BF16)],
        compiler_params=_cparams("arbitrary"),
        name="mem_kv",
    )(mem, g_mem, w_kv, g_k)


def _in_proj_kernel(x_ref, aq_ref, bq_ref, ak_ref, bk_ref, mkt_ref, mv_ref,
                    gmix_ref, w1_ref, gq_ref, wuq_ref, gkv_ref, wukv_ref, gmq_ref,
                    q_ref, k_ref, vt_ref, u_ref, om_ref):
    h = _rms(x_ref[0], gmix_ref[...], D_MODEL).astype(BF16)
    z = _dot(h, w1_ref[...])
    u_ref[0] = z[:, Z_U:Z_U + SSM_W]

    low_half = lax.broadcasted_iota(jnp.int32, (1, LANES), 1) < MEM_HD

    def memory_head(hd):
        off = Z_QM + hd * LANES
        mq = z[:, off:off + LANES]
        r = lax.rsqrt(jnp.sum(mq * mq, axis=-1, keepdims=True) * (1.0 / MEM_HD) + EPS)
        s = _dot((mq * gmq_ref[...] * r).astype(BF16), mkt_ref[0, hd])
        p = jnp.exp2(s - jnp.max(s, axis=-1, keepdims=True))
        o = _dot(p.astype(BF16), mv_ref[0, hd])
        one = _mem_ones_lane(hd)
        return o * (1.0 / o[:, one:one + 1])

    def memory_head_pair(pair):
        even, odd = memory_head(2 * pair), memory_head(2 * pair + 1)
        om_ref[0, :, pair * LANES:(pair + 1) * LANES] = jnp.where(low_half, even, odd).astype(BF16)

    cq = _rms(z[:, Z_CQ:Z_CQ + Q_LORA], gq_ref[...], Q_LORA).astype(BF16)
    q_all = _dot(cq, wuq_ref[...])
    ckv = _rms(z[:, Z_CKV:Z_CKV + KV_LORA], gkv_ref[...], KV_LORA).astype(BF16)
    kv = _dot(ckv, wukv_ref[...])

    aq, bq, ak, bk = aq_ref[...], bq_ref[...], ak_ref[...], bk_ref[...]
    k_rope = z[:, Z_KR:Z_KR + LANES]
    k_rope_rot = k_rope * ak + _swap_rope(k_rope) * bk
    real_lane = lax.broadcasted_iota(jnp.int32, (1, LANES), 1) < QK_DIM
    ss_rope = jnp.sum(jnp.where(real_lane, k_rope * k_rope, 0.0), axis=-1, keepdims=True)
    hw = MLA_HEADS * LANES

    for hd in range(MLA_HEADS):
        sl = slice(hd * LANES, (hd + 1) * LANES)
        qh = q_all[:, sl]
        r = _head_rsqrt_mxu(qh, QK_DIM)
        q_ref[0, hd] = ((qh * aq + _swap_rope(qh) * bq) * r).astype(BF16)
        kh = kv[:, sl]
        ss = jnp.sum(kh * kh, axis=-1, keepdims=True) + ss_rope
        r = lax.rsqrt(ss * (1.0 / QK_DIM) + EPS)
        k_ref[0, hd] = ((kh * ak + k_rope_rot) * r).astype(BF16)
        heads_per_pair = 2 * MLA_HEADS // MEM_HEADS
        if hd % heads_per_pair == 0:
            memory_head_pair(hd // heads_per_pair)

    v_t = kv[:, hw:hw + MLA_HEADS * V_HEAD].T.astype(BF16)
    tm = v_t.shape[1]
    ones_rows = jnp.where(lax.broadcasted_iota(jnp.int32, (VT_ROWS - V_HEAD, tm), 0) == 0, 1.0, 0.0)
    for hd in range(MLA_HEADS):
        vt_ref[0, hd, :V_HEAD, :] = v_t[hd * V_HEAD:(hd + 1) * V_HEAD, :]
        vt_ref[0, hd, V_HEAD:, :] = ones_rows.astype(BF16)


def _in_proj(x, rope, mkt, mv, wts):
    b, l, _ = x.shape
    tm = min(IN_PROJ_TILE, l)
    m = mkt.shape[-1]
    tile = lambda last: pl.BlockSpec((1, tm, last), lambda i, j: (i, j, 0))
    head_tile = pl.BlockSpec((1, MLA_HEADS, tm, LANES), lambda i, j: (i, 0, j, 0))
    rope_spec = pl.BlockSpec((tm, LANES), lambda i, j: (j, 0))
    return pl.pallas_call(
        _in_proj_kernel,
        grid=(b, l // tm),
        in_specs=[tile(D_MODEL), rope_spec, rope_spec, rope_spec, rope_spec,
                  pl.BlockSpec((1, MEM_HEADS, LANES, m), lambda i, j: (i, 0, 0, 0)),
                  pl.BlockSpec((1, MEM_HEADS, m, LANES), lambda i, j: (i, 0, 0, 0))]
                 + [_const_spec(w.shape) for w in wts],
        out_specs=[head_tile, head_tile,
                   pl.BlockSpec((1, MLA_HEADS, VT_ROWS, tm), lambda i, j: (i, 0, 0, j)),
                   tile(SSM_W), tile(MEM_W)],
        out_shape=[jax.ShapeDtypeStruct((b, MLA_HEADS, l, LANES), BF16),
                   jax.ShapeDtypeStruct((b, MLA_HEADS, l, LANES), BF16),
                   jax.ShapeDtypeStruct((b, MLA_HEADS, VT_ROWS, l), BF16),
                   jax.ShapeDtypeStruct((b, l, SSM_W), F32),
                   jax.ShapeDtypeStruct((b, l, MEM_W), BF16)],
        compiler_params=_cparams("arbitrary", "arbitrary"),
        name="in_proj",
    )(x, *rope, mkt, mv, *wts)


def _attn_kernel(bound_ref, q_ref, k_ref, vt_ref, o_ref):
    def run(subtract_max):
        outs = []
        for j in range(2):
            s_t = lax.dot_general(k_ref[0, j], q_ref[0, j], (((1,), (1,)), ((), ())),
                                  preferred_element_type=F32)
            if subtract_max:
                s_t = s_t - jnp.max(s_t, axis=0, keepdims=True)
            o_t = _dot(vt_ref[0, j], jnp.exp2(s_t).astype(BF16))
            outs.append(o_t[:V_HEAD] * (1.0 / o_t[V_HEAD:V_HEAD + 1]))
        o_ref[0] = jnp.concatenate(outs, axis=0).T.astype(BF16)

    no_shift_needed = bound_ref[0] <= SAFE_SCORE_LOG2
    pl.when(no_shift_needed)(lambda: run(False))
    pl.when(jnp.logical_not(no_shift_needed))(lambda: run(True))


def _attention(score_bound, q, k, vt):
    b, _, l, _ = q.shape
    tq = ATTN_Q_TILE
    return pl.pallas_call(
        _attn_kernel,
        grid=(b, MLA_HEADS // 2, l // tq),
        in_specs=[pl.BlockSpec(memory_space=pltpu.SMEM),
                  pl.BlockSpec((1, 2, tq, LANES), lambda i, h, j: (i, h, j, 0)),
                  pl.BlockSpec((1, 2, l, LANES), lambda i, h, j: (i, h, 0, 0)),
                  pl.BlockSpec((1, 2, VT_ROWS, l), lambda i, h, j: (i, h, 0, 0))],
        out_specs=pl.BlockSpec((1, tq, LANES), lambda i, h, j: (i, j, h)),
        out_shape=jax.ShapeDtypeStruct((b, l, MLA_HEADS * V_HEAD), BF16),
        compiler_params=_cparams("arbitrary", "arbitrary", "arbitrary"),
        name="mla_attn",
    )(score_bound, q, k, vt)


def _cpow(r, th, e):
    mag = jnp.exp(r * e)
    return mag * jnp.cos(th * e), mag * jnp.sin(th * e)


def _zoh(lr, li, dt):
    a_re, a_im = _cpow(lr * dt, li * dt, 1.0)
    den = lr * lr + li * li
    n_re = a_re - 1.0
    return (n_re * lr + a_im * li) / den, (a_im * lr - n_re * li) / den


def _cmul(ar, ai, br, bi):
    return ar * br - ai * bi, ar * bi + ai * br


def _ssm_prep_kernel(prow_ref, pcol_ref, bt_ref, btt_ref, ctt_ref,
                     m_ref, wst_ref, cp_ref, ap_ref):
    t_row = (lax.broadcasted_iota(jnp.int32, (SSM_TW, SSM_STATE), 0) // SSM_GROUP).astype(F32)
    t_lane = (lax.broadcasted_iota(jnp.int32, (SSM_STATE, SSM_TW), 1) // SSM_GROUP).astype(F32)
    lane = lax.broadcasted_iota(jnp.int32, (SSM_GROUP, SSM_TW), 1)
    k_row = lax.broadcasted_iota(jnp.int32, (2 * SSM_SCAN_STEPS + 2, LANES), 0)
    chunk_pow = (SSM_T * jnp.left_shift(1, k_row // 2)).astype(F32)
    odd_row = (k_row % 2) == 1
    lane_ap = lax.broadcasted_iota(jnp.int32, (2 * SSM_SCAN_STEPS + 2, LANES), 1)

    toeplitz = []
    for d in range(2):
        lr2, li2 = prow_ref[0, 3 * d:3 * d + 1, :], prow_ref[0, 3 * d + 1:3 * d + 2, :]
        dt2 = jnp.exp(prow_ref[0, 3 * d + 2:3 * d + 3, :])
        lr, li, dt = lr2[:, :SSM_STATE], li2[:, :SSM_STATE], dt2[:, :SSM_STATE]
        z_re, z_im = _zoh(lr, li, dt)

        p_re, p_im = _cpow(lr2 * dt2, li2 * dt2, chunk_pow)
        ap_ref[0, d] = jnp.where(odd_row, jnp.where(lane_ap < SSM_STATE, -p_im, p_im), p_re)

        e_in = (SSM_T - 1.0) - t_row if d == 0 else t_row
        w_re, w_im = _cmul(*_cpow(lr * dt, li * dt, e_in), z_re, z_im)
        b_re, b_im = btt_ref[0, d, 0], btt_ref[0, d, 1]
        s_re, s_im = _cmul(w_re, w_im, b_re, b_im)
        wst_ref[0, :, 2 * d * SSM_STATE:(2 * d + 1) * SSM_STATE] = s_re.astype(BF16)
        wst_ref[0, :, (2 * d + 1) * SSM_STATE:(2 * d + 2) * SSM_STATE] = s_im.astype(BF16)

        lrc, lic = pcol_ref[0, :, 3 * d:3 * d + 1], pcol_ref[0, :, 3 * d + 1:3 * d + 2]
        dtc = jnp.exp(pcol_ref[0, :, 3 * d + 2:3 * d + 3])
        c_re, c_im = ctt_ref[0, d, 0], ctt_ref[0, d, 1]

        e_out = t_lane + 1.0 if d == 0 else SSM_T - t_lane
        g_re, g_im = _cmul(*_cpow(lrc * dtc, lic * dtc, e_out), c_re, c_im)
        cp_ref[0, 2 * d * SSM_STATE:(2 * d + 1) * SSM_STATE, :] = g_re.astype(BF16)
        cp_ref[0, (2 * d + 1) * SSM_STATE:(2 * d + 2) * SSM_STATE, :] = (-g_im).astype(BF16)

        e_lag = t_lane if d == 0 else (SSM_T - 1.0) - t_lane
        k_re, k_im = _cmul(*_cpow(lrc * dtc, lic * dtc, e_lag), c_re, c_im)
        bb_re, bb_im = _cmul(z_re, z_im, bt_ref[0, d, 0], bt_ref[0, d, 1])
        lag = (jnp.dot(bb_re, k_re, preferred_element_type=F32, precision=lax.Precision.HIGHEST)
               - jnp.dot(bb_im, k_im, preferred_element_type=F32, precision=lax.Precision.HIGHEST))
        toeplitz.append(lag)

    lag_f, lag_b = toeplitz
    for s in range(SSM_T):
        fwd = lag_f if s == 0 else pltpu.roll(lag_f, SSM_GROUP * s, 1)
        fwd = jnp.where(lane >= SSM_GROUP * s, fwd, 0.0)
        shift = (SSM_TW - SSM_GROUP * (SSM_T - 1 - s)) % SSM_TW
        bwd = lag_b if shift == 0 else pltpu.roll(lag_b, shift, 1)
        bwd = jnp.where(lane < SSM_GROUP * (s + 1), bwd, 0.0)
        m_ref[0, s * SSM_GROUP:(s + 1) * SSM_GROUP, :] = (fwd + bwd).astype(BF16)


def _ssm_prep(lam_re, lam_im, log_step, b_re, b_im, c_re, c_im):
    g, p, hg, t = SSM_GROUPS, SSM_STATE, SSM_GROUP, SSM_T
    step = jnp.broadcast_to(log_step[:, :, None], (2, g, p))
    rows = jnp.stack([lam_re[0], lam_im[0], step[0], lam_re[1], lam_im[1], step[1]], axis=1)
    rows = jnp.pad(rows, ((0, 0), (0, 2), (0, 0)))
    prow = jnp.concatenate([rows, rows], axis=-1)
    pcol = rows.transpose(0, 2, 1)
    bt = jnp.stack([b_re, b_im], axis=1).transpose(2, 0, 1, 4, 3)
    btt = jnp.tile(bt, (1, 1, 1, t, 1))
    ct = jnp.stack([c_re, c_im], axis=1).transpose(2, 0, 1, 4, 3)
    ctt = jnp.tile(ct, (1, 1, 1, 1, t))
    n_ap = 2 * SSM_SCAN_STEPS + 2
    blk = lambda *s: pl.BlockSpec((1,) + s, lambda i: (i,) + (0,) * len(s))
    return pl.pallas_call(
        _ssm_prep_kernel,
        grid=(g,),
        in_specs=[blk(8, LANES), blk(p, 8), blk(2, 2, hg, p), blk(2, 2, t * hg, p), blk(2, 2, p, t * hg)],
        out_specs=[blk(SSM_TW, SSM_TW), blk(SSM_TW, 4 * p), blk(4 * p, SSM_TW), blk(2, n_ap, LANES)],
        out_shape=[jax.ShapeDtypeStruct((g, SSM_TW, SSM_TW), BF16),
                   jax.ShapeDtypeStruct((g, SSM_TW, 4 * p), BF16),
                   jax.ShapeDtypeStruct((g, 4 * p, SSM_TW), BF16),
                   jax.ShapeDtypeStruct((g, 2, n_ap, LANES), F32)],
        compiler_params=_cparams("arbitrary"),
        name="ssm_prep",
    )(prow, pcol, bt, btt, ctt)


def _atom_transpose_perm():
    i = jnp.arange(SSM_PERM_W)
    a, b, h = i // LANES, (i % LANES) // SSM_GROUP, i % SSM_GROUP
    j = b * LANES + a * SSM_GROUP + h
    return (j[:, None] == i[None, :]).astype(BF16)


def _ssm_kernel(u_lo_ref, u_hi_ref, perm_ref, m_ref, wst_ref, cp_ref, ap_ref, y_lo_ref, y_hi_ref,
                ug_ref, yg_ref, z_ref, s_ref):
    u_refs, y_refs = (u_lo_ref, u_hi_ref), (y_lo_ref, y_hi_ref)
    nc = ug_ref.shape[1]
    perm = perm_ref[...]
    blocks = [(tb, half) for tb in range(SSM_T // SSM_PER_TILE) for half in range(SSM_GROUPS // SSM_PER_TILE)]

    def token_rows(refs, tb, t, half):
        return refs[half].at[0, pl.ds(tb * SSM_PER_TILE + t, nc, stride=SSM_T), :]

    for tb, half in blocks:
        x = jnp.concatenate([token_rows(u_refs, tb, t, half)[...] for t in range(SSM_PER_TILE)], axis=1)
        r = _dot(x.astype(BF16), perm).astype(BF16)
        for gi in range(SSM_PER_TILE):
            ug_ref[half * SSM_PER_TILE + gi, :, tb * LANES:(tb + 1) * LANES] = r[:, gi * LANES:(gi + 1) * LANES]

    for g in range(SSM_GROUPS):
        z_ref[g] = _dot(ug_ref[g], wst_ref[g])

    rows = SSM_GROUPS * nc
    chunk = lax.broadcasted_iota(jnp.int32, (rows, LANES), 0) & (nc - 1)

    def shifted(s3, sh, d):
        flat = s3.reshape(rows, LANES)
        if d == 0:
            return jnp.where(chunk >= sh, pltpu.roll(flat, sh, 0), 0.0)
        return jnp.where(chunk < nc - sh, pltpu.roll(flat, rows - sh, 0), 0.0)

    as3d = lambda a: a.reshape(SSM_GROUPS, nc, LANES)
    for d in range(2):
        s3 = z_ref[:, :, d * LANES:(d + 1) * LANES]
        for k in range(nc.bit_length() - 1):
            t = shifted(s3, 1 << k, d)
            s3 = s3 + ap_ref[:, d, 2 * k:2 * k + 1, :] * as3d(t) \
                + ap_ref[:, d, 2 * k + 1:2 * k + 2, :] * as3d(pltpu.roll(t, SSM_STATE, 1))
        s_ref[:, :, d * LANES:(d + 1) * LANES] = as3d(shifted(s3, 1, d)).astype(BF16)

    for g in range(SSM_GROUPS):
        yg_ref[g] = _dot(ug_ref[g], m_ref[g]) + _dot(s_ref[g], cp_ref[g])

    for tb, half in blocks:
        yv = jnp.concatenate([yg_ref[half * SSM_PER_TILE + gi, :, tb * LANES:(tb + 1) * LANES]
                              for gi in range(SSM_PER_TILE)], axis=1)
        hi = yv.astype(BF16)
        lo = (yv - hi.astype(F32)).astype(BF16)
        out = _dot(hi, perm) + _dot(lo, perm)
        for t in range(SSM_PER_TILE):
            token_rows(y_refs, tb, t, half)[...] = out[:, t * LANES:(t + 1) * LANES]


def _ssm(u, tables):
    m, wst, cp, ap = tables
    b, l, w = u.shape
    nc = l // SSM_T
    assert nc & (nc - 1) == 0 and nc <= 1 << SSM_SCAN_STEPS, "chunk count must be a power of two"
    perm = _atom_transpose_perm()
    assert w == 2 * LANES
    half = lambda j: pl.BlockSpec((1, l, LANES), lambda i: (i, 0, j))
    return pl.pallas_call(
        _ssm_kernel,
        grid=(b,),
        in_specs=[half(0), half(1)] + [_const_spec(a.shape) for a in (perm, m, wst, cp, ap)],
        out_specs=[half(0), half(0)],
        out_shape=[jax.ShapeDtypeStruct((b, l, LANES), F32)] * 2,
        scratch_shapes=[pltpu.VMEM((SSM_GROUPS, nc, SSM_TW), BF16),
                        pltpu.VMEM((SSM_GROUPS, nc, SSM_TW), F32),
                        pltpu.VMEM((SSM_GROUPS, nc, 2 * LANES), F32),
                        pltpu.VMEM((SSM_GROUPS, nc, 2 * LANES), BF16)],
        compiler_params=_cparams("arbitrary"),
        name="ssm_scan",
    )(u, u, perm, m, wst, cp, ap)


def _gelu_tanh(x):
    return 0.5 * x * (1.0 + jnp.tanh(math.sqrt(2.0 / math.pi) * (x + 0.044715 * (x * x * x))))


def _merge_kernel(x_ref, oa_ref, y_lo_ref, y_hi_ref, u_ref, om_ref,
                  gmix_ref, wg_ref, d_ref, wglu_ref, wa_ref, ws_ref, wm_ref, wout_ref, o_ref):
    x = x_ref[...]
    h = _rms(x, gmix_ref[...], D_MODEL).astype(BF16)
    y = jnp.concatenate([y_lo_ref[...], y_hi_ref[...]], axis=1)
    y = _gelu_tanh(y + d_ref[...] * u_ref[...])
    o_ssm = (y * _sigmoid(_dot(y.astype(BF16), wglu_ref[...]))).astype(BF16)
    branches = ((oa_ref[...], wa_ref), (o_ssm, ws_ref), (om_ref[...], wm_ref))
    merged = None
    for i, (o_b, w_ref) in enumerate(branches):
        gate = _sigmoid(_dot(h, wg_ref[:, i * D_MODEL:(i + 1) * D_MODEL]))
        term = gate * _dot(o_b, w_ref[...])
        merged = term if merged is None else merged + term
    o_ref[...] = x + _dot(merged.astype(BF16), wout_ref[...])


def _merge(x2, oa2, y_lo, y_hi, u2, om2, wts):
    n = x2.shape[0]
    tm = TOKEN_TILE
    tile = lambda a: pl.BlockSpec((tm, a.shape[1]), lambda i: (i, 0))
    return pl.pallas_call(
        _merge_kernel,
        grid=(n // tm,),
        in_specs=[tile(x2), tile(oa2), tile(y_lo), tile(y_hi), tile(u2), tile(om2)]
                 + [_const_spec(w.shape) for w in wts],
        out_specs=tile(x2),
        out_shape=jax.ShapeDtypeStruct(x2.shape, F32),
        compiler_params=_cparams("arbitrary"),
        name="merge_out",
    )(x2, oa2, y_lo, y_hi, u2, om2, *wts)


def _ffn_kernel(x_ref, g_ref, wgate_ref, wup_ref, wdown_ref, o_ref):
    x = x_ref[...]
    h = _rms(x, g_ref[...], D_MODEL).astype(BF16)
    gate = _dot(h, wgate_ref[...])
    act = (gate * _sigmoid(gate) * _dot(h, wup_ref[...])).astype(BF16)
    o_ref[...] = x + _dot(act, wdown_ref[...])


def _ffn(x2, wts):
    n = x2.shape[0]
    tm = TOKEN_TILE
    tile = pl.BlockSpec((tm, D_MODEL), lambda i: (i, 0))
    return pl.pallas_call(
        _ffn_kernel,
        grid=(n // tm,),
        in_specs=[tile] + [_const_spec(w.shape) for w in wts],
        out_specs=tile,
        out_shape=jax.ShapeDtypeStruct(x2.shape, F32),
        compiler_params=_cparams("arbitrary"),
        name="ffn",
    )(x2, *wts)


def _pad_heads(w, heads, width):
    lead = w.shape[:-1]
    w = w.reshape(lead + (heads, width))
    w = jnp.pad(w, [(0, 0)] * len(lead) + [(0, 0), (0, LANES - width)])
    return w.reshape(lead + (heads * LANES,))


def _dup_first_rope_half(w):
    return jnp.concatenate([w[..., :QK_DIM], w[..., ROPE_LO:ROPE_HI],
                            w[..., QK_DIM + HALF_ROPE:]], axis=-1)


def _pad_row(g):
    return jnp.pad(g, (0, LANES - g.shape[0])).reshape(1, LANES)


def _rope_tables(length, gain, scale):
    inv = 1.0 / (ROPE_THETA ** (jnp.arange(0, QK_ROPE, 2, dtype=F32) / QK_ROPE))
    ang = jnp.arange(length, dtype=F32)[:, None] * inv[None, :]
    cos, sin = jnp.cos(ang), jnp.sin(ang)
    g = gain * scale
    g_nope, g_lo, g_hi = g[:QK_NOPE], g[ROPE_LO:ROPE_HI], g[ROPE_HI:QK_DIM]
    pad = jnp.zeros((length, LANES - QK_DIM), F32)
    a = jnp.concatenate([jnp.broadcast_to(g_nope, (length, QK_NOPE)), cos * g_lo, cos * g_hi, pad], axis=1)
    b = jnp.concatenate([jnp.zeros((length, QK_NOPE), F32), -sin * g_hi, sin * g_lo, pad], axis=1)
    return a, b


def _layer(x, mem, rope, score_bound, p):
    b, l, d = x.shape
    mkt, mv = _mem_kv(mem, p["g_mem"], p["w_mem_kv"], p["g_mk"])
    q, k, vt, u, o_mem = _in_proj(x, rope, mkt, mv, p["in_proj"])
    o_attn = _attention(score_bound, q, k, vt)

    y_lo, y_hi = (y.reshape(b * l, LANES) for y in _ssm(u, p["ssm"]))

    x2 = x.reshape(b * l, d)
    x1 = _merge(x2, o_attn.reshape(b * l, -1), y_lo, y_hi, u.reshape(b * l, SSM_W),
                o_mem.reshape(b * l, -1), p["merge"])
    return _ffn(x1, p["ffn"]).reshape(b, l, d)


def kernel(x_prompt, x_sample, mem_prompt, mem_sample, norm_mix_g, norm_mem_g, w_in, q_lora_norm_g, kv_lora_norm_g, w_uq, w_ukv, mla_q_norm_g, mla_k_norm_g, ssm_lambda_re, ssm_lambda_im, ssm_log_step, ssm_b_re, ssm_b_im, ssm_c_re, ssm_c_im, ssm_d, ssm_w_glu, w_mem_kv, mem_q_norm_g, mem_k_norm_g, w_br_attn, w_br_ssm, w_br_mem, w_out, norm_ffn_g, w_ffn_gate, w_ffn_up, w_ffn_down):
    assert norm_mix_g.shape[0] == 1, "single-layer encoder"
    row = lambda g: g.reshape(1, -1)
    w_in0 = w_in[0]

    k_rope_cols = jnp.pad(w_in0[:, OFF_KV:OFF_KR], ((0, 0), (QK_NOPE, LANES - QK_DIM)))
    w1 = jnp.concatenate([w_in0[:, :OFF_KV], _dup_first_rope_half(k_rope_cols),
                          w_in0[:, OFF_KR:OFF_SSM],
                          _pad_heads(w_in0[:, OFF_SSM:OFF_MEM], MEM_HEADS, MEM_HD)], axis=1).astype(BF16)
    w_uq_p = _pad_heads(w_uq[0], MLA_HEADS, QK_DIM).reshape(Q_LORA, MLA_HEADS, LANES)
    w_uq_p = _dup_first_rope_half(w_uq_p).reshape(Q_LORA, -1).astype(BF16)
    w_ukv_h = w_ukv[0].reshape(KV_LORA, MLA_HEADS, QK_NOPE + V_HEAD)
    w_k = _pad_heads(w_ukv_h[..., :QK_NOPE].reshape(KV_LORA, -1), MLA_HEADS, QK_NOPE)
    w_v = w_ukv_h[..., QK_NOPE:].reshape(KV_LORA, -1)
    w_ukv_p = jnp.concatenate([w_k, w_v], axis=1).astype(BF16)

    w_mkv = w_mem_kv[0]
    w_mv = _pad_heads(w_mkv[:, MEM_W:], MEM_HEADS, MEM_HD).reshape(D_MODEL, MEM_HEADS, LANES)
    w_mv = jnp.where((jnp.arange(MEM_HEADS) % 2 == 1)[None, :, None], jnp.roll(w_mv, MEM_HD, axis=-1), w_mv)
    w_mkv_p = jnp.concatenate([_pad_heads(w_mkv[:, :MEM_W], MEM_HEADS, MEM_HD),
                               w_mv.reshape(D_MODEL, -1)], axis=1).astype(BF16)

    params = {
        "g_mem": row(norm_mem_g[0]), "w_mem_kv": w_mkv_p, "g_mk": _pad_row(mem_k_norm_g[0]),
        "in_proj": (row(norm_mix_g[0]), w1, row(q_lora_norm_g[0]), w_uq_p, row(kv_lora_norm_g[0]), w_ukv_p,
                    _pad_row(mem_q_norm_g[0] * MEM_Q_SCALE)),
        "ssm": _ssm_prep(ssm_lambda_re[0], ssm_lambda_im[0], ssm_log_step[0],
                         ssm_b_re[0], ssm_b_im[0], ssm_c_re[0], ssm_c_im[0]),
        "merge": (row(norm_mix_g[0]), w_in0[:, OFF_MEM:].astype(BF16), row(ssm_d[0]),
                  ssm_w_glu[0].astype(BF16), w_br_attn[0].astype(BF16), w_br_ssm[0].astype(BF16),
                  w_br_mem[0].astype(BF16), w_out[0].astype(BF16)),
        "ffn": (row(norm_ffn_g[0]), w_ffn_gate[0].astype(BF16), w_ffn_up[0].astype(BF16),
                w_ffn_down[0].astype(BF16)),
    }
    score_bound = (1.02 * QK_DIM * Q_SCALE * jnp.max(jnp.abs(mla_q_norm_g[0]))
                   * jnp.max(jnp.abs(mla_k_norm_g[0]))).reshape(1).astype(F32)

    def rope(length):
        return (_rope_tables(length, mla_q_norm_g[0], Q_SCALE)
                + _rope_tables(length, mla_k_norm_g[0], 1.0))

    rope_p = rope(x_prompt.shape[1])
    rope_s = rope_p if x_sample.shape[1] == x_prompt.shape[1] else rope(x_sample.shape[1])
    y_prompt = _layer(x_prompt, mem_prompt, rope_p, score_bound, params)
    y_sample = _layer(x_sample, mem_sample, rope_s, score_bound, params)
    return (y_prompt, y_sample)
```

```python
import functools
import math

import jax
import jax.numpy as jnp
from jax import lax
from jax.experimental import pallas as pl
from jax.experimental.pallas import tpu as pltpu

F32 = jnp.float32
BF16 = jnp.bfloat16

D_MODEL = 1024
MLA_HEADS = 8
QK_NOPE = 64
QK_ROPE = 32
QK_DIM = QK_NOPE + QK_ROPE
V_HEAD = 64
Q_LORA = 384
KV_LORA = 256
ROPE_THETA = 10000.0
SSM_GROUP = 16
SSM_W = 256
SSM_GROUPS = SSM_W // SSM_GROUP
SSM_STATE = 64
MEM_HEADS = 4
MEM_HD = 64
MEM_W = MEM_HEADS * MEM_HD
N_BRANCH = 3
FF = -(-8 * D_MODEL // (3 * 256)) * 256
EPS = 1e-6
OFF_Q = Q_LORA
OFF_KV = OFF_Q + KV_LORA
OFF_KR = OFF_KV + QK_ROPE
OFF_SSM = OFF_KR + SSM_W
OFF_MEM = OFF_SSM + MEM_W

LANES = 128
HALF_ROPE = QK_ROPE // 2
ROPE_LO = QK_NOPE
ROPE_HI = QK_NOPE + HALF_ROPE
LOG2E = 1.4426950408889634
Q_SCALE = (QK_DIM ** -0.5) * LOG2E
MEM_Q_SCALE = (MEM_HD ** -0.5) * LOG2E
SAFE_SCORE_LOG2 = 60.0

Z_CQ = 0
Z_CKV = Z_CQ + Q_LORA
Z_KR = Z_CKV + KV_LORA
Z_U = Z_KR + LANES
Z_QM = Z_U + SSM_W
Z_END = Z_QM + MEM_HEADS * LANES

SSM_T = 32
SSM_TW = SSM_T * SSM_GROUP
SSM_SCAN_STEPS = 7
SSM_PER_TILE = LANES // SSM_GROUP
SSM_PERM_W = SSM_PER_TILE * LANES

VT_ROWS = V_HEAD + 16

TOKEN_TILE = 512
IN_PROJ_TILE = 1024
ATTN_Q_TILE = 512
VMEM_LIMIT = 52 * 1024 * 1024


def _cparams(*sem):
    return pltpu.CompilerParams(dimension_semantics=sem, vmem_limit_bytes=VMEM_LIMIT)


def _const_spec(shape):
    n = len(shape)
    return pl.BlockSpec(shape, lambda *_: (0,) * n, pipeline_mode=pl.Buffered(1))


def _rms(x, g, n):
    ms = jnp.sum(x * x, axis=-1, keepdims=True) * (1.0 / n)
    return x * lax.rsqrt(ms + EPS) * g


def _sigmoid(x):
    return 1.0 / (1.0 + jnp.exp(-x))


def _dot(a, b):
    return jnp.dot(a, b, preferred_element_type=F32)


def _lane_onehot(lane_idx):
    return jnp.where(lax.broadcasted_iota(jnp.int32, (1, LANES), 1) == lane_idx, 1.0, 0.0)


def _mem_ones_lane(head):
    return MEM_HD if head % 2 == 0 else 0


def _head_rsqrt_mxu(t, n_valid):
    ones = jnp.where(lax.broadcasted_iota(jnp.int32, (LANES, LANES), 0) < n_valid, 1.0, 0.0)
    ss = _dot((t * t).astype(BF16), ones.astype(BF16))
    return lax.rsqrt(ss * (1.0 / n_valid) + EPS)


def _swap_rope(t):
    return pltpu.roll(t, LANES - HALF_ROPE, 1)


def _mem_kv_kernel(mem_ref, g_ref, w_ref, gk_ref, mkt_ref, mv_ref):
    m = _rms(mem_ref[0], g_ref[...], D_MODEL).astype(BF16)
    kv = _dot(m, w_ref[...])
    for h in range(MEM_HEADS):
        k = _rms(kv[:, h * LANES:(h + 1) * LANES], gk_ref[...], MEM_HD)
        mkt_ref[0, h] = k.T.astype(BF16)
        off = (MEM_HEADS + h) * LANES
        mv_ref[0, h] = (kv[:, off:off + LANES] + _lane_onehot(_mem_ones_lane(h))).astype(BF16)


def _mem_kv(mem, g_mem, w_kv, g_k):
    b, m, _ = mem.shape
    return pl.pallas_call(
        _mem_kv_kernel,
        grid=(b,),
        in_specs=[pl.BlockSpec((1, m, D_MODEL), lambda i: (i, 0, 0)),
                  _const_spec((1, D_MODEL)),
                  _const_spec((D_MODEL, 2 * MEM_HEADS * LANES)),
                  _const_spec((1, LANES))],
        out_specs=[pl.BlockSpec((1, MEM_HEADS, LANES, m), lambda i: (i, 0, 0, 0)),
                   pl.BlockSpec((1, MEM_HEADS, m, LANES), lambda i: (i, 0, 0, 0))],
        out_shape=[jax.ShapeDtypeStruct((b, MEM_HEADS, LANES, m), BF16),
                   jax.ShapeDtypeStruct((b, MEM_HEADS, m, LANES), BF16)],
        compiler_params=_cparams("arbitrary"),
        name="mem_kv",
    )(mem, g_mem, w_kv, g_k)


def _in_proj_kernel(x_ref, aq_ref, bq_ref, ak_ref, bk_ref, mkt_ref, mv_ref,
                    gmix_ref, w1_ref, gq_ref, wuq_ref, gkv_ref, wukv_ref, gmq_ref,
                    q_ref, k_ref, vt_ref, u_ref, om_ref):
    h = _rms(x_ref[0], gmix_ref[...], D_MODEL).astype(BF16)
    z = _dot(h, w1_ref[...])
    u_ref[0] = z[:, Z_U:Z_U + SSM_W]

    low_half = lax.broadcasted_iota(jnp.int32, (1, LANES), 1) < MEM_HD

    def memory_head(hd):
        off = Z_QM + hd * LANES
        mq = z[:, off:off + LANES]
        r = lax.rsqrt(jnp.sum(mq * mq, axis=-1, keepdims=True) * (1.0 / MEM_HD) + EPS)
        s = _dot((mq * gmq_ref[...] * r).astype(BF16), mkt_ref[0, hd])
        p = jnp.exp2(s - jnp.max(s, axis=-1, keepdims=True))
        o = _dot(p.astype(BF16), mv_ref[0, hd])
        one = _mem_ones_lane(hd)
        return o * (1.0 / o[:, one:one + 1])

    def memory_head_pair(pair):
        even, odd = memory_head(2 * pair), memory_head(2 * pair + 1)
        om_ref[0, :, pair * LANES:(pair + 1) * LANES] = jnp.where(low_half, even, odd).astype(BF16)

    cq = _rms(z[:, Z_CQ:Z_CQ + Q_LORA], gq_ref[...], Q_LORA).astype(BF16)
    q_all = _dot(cq, wuq_ref[...])
    ckv = _rms(z[:, Z_CKV:Z_CKV + KV_LORA], gkv_ref[...], KV_LORA).astype(BF16)
    kv = _dot(ckv, wukv_ref[...])

    aq, bq, ak, bk = aq_ref[...], bq_ref[...], ak_ref[...], bk_ref[...]
    k_rope = z[:, Z_KR:Z_KR + LANES]
    k_rope_rot = k_rope * ak + _swap_rope(k_rope) * bk
    real_lane = lax.broadcasted_iota(jnp.int32, (1, LANES), 1) < QK_DIM
    ss_rope = jnp.sum(jnp.where(real_lane, k_rope * k_rope, 0.0), axis=-1, keepdims=True)
    hw = MLA_HEADS * LANES

    for hd in range(MLA_HEADS):
        sl = slice(hd * LANES, (hd + 1) * LANES)
        qh = q_all[:, sl]
        r = _head_rsqrt_mxu(qh, QK_DIM)
        q_ref[0, hd] = ((qh * aq + _swap_rope(qh) * bq) * r).astype(BF16)
        kh = kv[:, sl]
        ss = jnp.sum(kh * kh, axis=-1, keepdims=True) + ss_rope
        r = lax.rsqrt(ss * (1.0 / QK_DIM) + EPS)
        k_ref[0, hd] = ((kh * ak + k_rope_rot) * r).astype(BF16)
        heads_per_pair = 2 * MLA_HEADS // MEM_HEADS
        if hd % heads_per_pair == 0:
            memory_head_pair(hd // heads_per_pair)

    v_t = kv[:, hw:hw + MLA_HEADS * V_HEAD].T.astype(BF16)
    tm = v_t.shape[1]
    ones_rows = jnp.where(lax.broadcasted_iota(jnp.int32, (VT_ROWS - V_HEAD, tm), 0) == 0, 1.0, 0.0)
    for hd in range(MLA_HEADS):
        vt_ref[0, hd, :V_HEAD, :] = v_t[hd * V_HEAD:(hd + 1) * V_HEAD, :]
        vt_ref[0, hd, V_HEAD:, :] = ones_rows.astype(BF16)


def _in_proj(x, rope, mkt, mv, wts):
    b, l, _ = x.shape
    tm = min(IN_PROJ_TILE, l)
    m = mkt.shape[-1]
    tile = lambda last: pl.BlockSpec((1, tm, last), lambda i, j: (i, j, 0))
    head_tile = pl.BlockSpec((1, MLA_HEADS, tm, LANES), lambda i, j: (i, 0, j, 0))
    rope_spec = pl.BlockSpec((tm, LANES), lambda i, j: (j, 0))
    return pl.pallas_call(
        _in_proj_kernel,
        grid=(b, l // tm),
        in_specs=[tile(D_MODEL), rope_spec, rope_spec, rope_spec, rope_spec,
                  pl.BlockSpec((1, MEM_HEADS, LANES, m), lambda i, j: (i, 0, 0, 0)),
                  pl.BlockSpec((1, MEM_HEADS, m, LANES), lambda i, j: (i, 0, 0, 0))]
                 + [_const_spec(w.shape) for w in wts],
        out_specs=[head_tile, head_tile,
                   pl.BlockSpec((1, MLA_HEADS, VT_ROWS, tm), lambda i, j: (i, 0, 0, j)),
                   tile(SSM_W), tile(MEM_W)],
        out_shape=[jax.ShapeDtypeStruct((b, MLA_HEADS, l, LANES), BF16),
                   jax.ShapeDtypeStruct((b, MLA_HEADS, l, LANES), BF16),
                   jax.ShapeDtypeStruct((b, MLA_HEADS, VT_ROWS, l), BF16),
                   jax.ShapeDtypeStruct((b, l, SSM_W), F32),
                   jax.ShapeDtypeStruct((b, l, MEM_W), BF16)],
        compiler_params=_cparams("arbitrary", "arbitrary"),
        name="in_proj",
    )(x, *rope, mkt, mv, *wts)


def _attn_kernel(q_ref, k_ref, vt_ref, o_ref, *, subtract_max):
    outs = []
    for j in range(2):
        s_t = lax.dot_general(k_ref[0, j], q_ref[0, j], (((1,), (1,)), ((), ())),
                              preferred_element_type=F32)
        if subtract_max:
            s_t = s_t - jnp.max(s_t, axis=0, keepdims=True)
        o_t = _dot(vt_ref[0, j], jnp.exp2(s_t).astype(BF16))
        outs.append(o_t[:V_HEAD] * (1.0 / o_t[V_HEAD:V_HEAD + 1]))
    o_ref[0] = jnp.concatenate(outs, axis=0).T.astype(BF16)


def _attention(score_bound, q, k, vt):
    b, _, l, _ = q.shape
    tq = ATTN_Q_TILE

    def call(subtract_max):
        return pl.pallas_call(
            functools.partial(_attn_kernel, subtract_max=subtract_max),
            grid=(b, MLA_HEADS // 2, l // tq),
            in_specs=[pl.BlockSpec((1, 2, tq, LANES), lambda i, h, j: (i, h, j, 0)),
                      pl.BlockSpec((1, 2, l, LANES), lambda i, h, j: (i, h, 0, 0)),
                      pl.BlockSpec((1, 2, VT_ROWS, l), lambda i, h, j: (i, h, 0, 0))],
            out_specs=pl.BlockSpec((1, tq, LANES), lambda i, h, j: (i, j, h)),
            out_shape=jax.ShapeDtypeStruct((b, l, MLA_HEADS * V_HEAD), BF16),
            compiler_params=_cparams("arbitrary", "arbitrary", "arbitrary"),
            name="mla_attn_shifted" if subtract_max else "mla_attn",
        )

    return lax.cond(score_bound <= SAFE_SCORE_LOG2, call(False), call(True), q, k, vt)


def _cpow(r, th, e):
    mag = jnp.exp(r * e)
    return mag * jnp.cos(th * e), mag * jnp.sin(th * e)


def _zoh(lr, li, dt):
    a_re, a_im = _cpow(lr * dt, li * dt, 1.0)
    den = lr * lr + li * li
    n_re = a_re - 1.0
    return (n_re * lr + a_im * li) / den, (a_im * lr - n_re * li) / den


def _cmul(ar, ai, br, bi):
    return ar * br - ai * bi, ar * bi + ai * br


def _ssm_prep_kernel(prow_ref, pcol_ref, bt_ref, btt_ref, ctt_ref,
                     m_ref, wst_ref, cp_ref, ap_ref):
    t_row = (lax.broadcasted_iota(jnp.int32, (SSM_TW, SSM_STATE), 0) // SSM_GROUP).astype(F32)
    t_lane = (lax.broadcasted_iota(jnp.int32, (SSM_STATE, SSM_TW), 1) // SSM_GROUP).astype(F32)
    lane = lax.broadcasted_iota(jnp.int32, (SSM_GROUP, SSM_TW), 1)
    k_row = lax.broadcasted_iota(jnp.int32, (2 * SSM_SCAN_STEPS + 2, LANES), 0)
    chunk_pow = (SSM_T * jnp.left_shift(1, k_row // 2)).astype(F32)
    odd_row = (k_row % 2) == 1
    lane_ap = lax.broadcasted_iota(jnp.int32, (2 * SSM_SCAN_STEPS + 2, LANES), 1)

    toeplitz = []
    for d in range(2):
        lr2, li2 = prow_ref[0, 3 * d:3 * d + 1, :], prow_ref[0, 3 * d + 1:3 * d + 2, :]
        dt2 = jnp.exp(prow_ref[0, 3 * d + 2:3 * d + 3, :])
        lr, li, dt = lr2[:, :SSM_STATE], li2[:, :SSM_STATE], dt2[:, :SSM_STATE]
        z_re, z_im = _zoh(lr, li, dt)

        p_re, p_im = _cpow(lr2 * dt2, li2 * dt2, chunk_pow)
        ap_ref[0, d] = jnp.where(odd_row, jnp.where(lane_ap < SSM_STATE, -p_im, p_im), p_re)

        e_in = (SSM_T - 1.0) - t_row if d == 0 else t_row
        w_re, w_im = _cmul(*_cpow(lr * dt, li * dt, e_in), z_re, z_im)
        b_re, b_im = btt_ref[0, d, 0], btt_ref[0, d, 1]
        s_re, s_im = _cmul(w_re, w_im, b_re, b_im)
        wst_ref[0, :, 2 * d * SSM_STATE:(2 * d + 1) * SSM_STATE] = s_re.astype(BF16)
        wst_ref[0, :, (2 * d + 1) * SSM_STATE:(2 * d + 2) * SSM_STATE] = s_im.astype(BF16)

        lrc, lic = pcol_ref[0, :, 3 * d:3 * d + 1], pcol_ref[0, :, 3 * d + 1:3 * d + 2]
        dtc = jnp.exp(pcol_ref[0, :, 3 * d + 2:3 * d + 3])
        c_re, c_im = ctt_ref[0, d, 0], ctt_ref[0, d, 1]

        e_out = t_lane + 1.0 if d == 0 else SSM_T - t_lane
        g_re, g_im = _cmul(*_cpow(lrc * dtc, lic * dtc, e_out), c_re, c_im)
        cp_ref[0, 2 * d * SSM_STATE:(2 * d + 1) * SSM_STATE, :] = g_re.astype(BF16)
        cp_ref[0, (2 * d + 1) * SSM_STATE:(2 * d + 2) * SSM_STATE, :] = (-g_im).astype(BF16)

        e_lag = t_lane if d == 0 else (SSM_T - 1.0) - t_lane
        k_re, k_im = _cmul(*_cpow(lrc * dtc, lic * dtc, e_lag), c_re, c_im)
        bb_re, bb_im = _cmul(z_re, z_im, bt_ref[0, d, 0], bt_ref[0, d, 1])
        lag = (jnp.dot(bb_re, k_re, preferred_element_type=F32, precision=lax.Precision.HIGHEST)
               - jnp.dot(bb_im, k_im, preferred_element_type=F32, precision=lax.Precision.HIGHEST))
        toeplitz.append(lag)

    lag_f, lag_b = toeplitz
    for s in range(SSM_T):
        fwd = lag_f if s == 0 else pltpu.roll(lag_f, SSM_GROUP * s, 1)
        fwd = jnp.where(lane >= SSM_GROUP * s, fwd, 0.0)
        shift = (SSM_TW - SSM_GROUP * (SSM_T - 1 - s)) % SSM_TW
        bwd = lag_b if shift == 0 else pltpu.roll(lag_b, shift, 1)
        bwd = jnp.where(lane < SSM_GROUP * (s + 1), bwd, 0.0)
        m_ref[0, s * SSM_GROUP:(s + 1) * SSM_GROUP, :] = (fwd + bwd).astype(BF16)


def _ssm_prep(lam_re, lam_im, log_step, b_re, b_im, c_re, c_im):
    g, p, hg, t = SSM_GROUPS, SSM_STATE, SSM_GROUP, SSM_T
    step = jnp.broadcast_to(log_step[:, :, None], (2, g, p))
    rows = jnp.stack([lam_re[0], lam_im[0], step[0], lam_re[1], lam_im[1], step[1]], axis=1)
    rows = jnp.pad(rows, ((0, 0), (0, 2), (0, 0)))
    prow = jnp.concatenate([rows, rows], axis=-1)
    pcol = rows.transpose(0, 2, 1)
    bt = jnp.stack([b_re, b_im], axis=1).transpose(2, 0, 1, 4, 3)
    btt = jnp.tile(bt, (1, 1, 1, t, 1))
    ct = jnp.stack([c_re, c_im], axis=1).transpose(2, 0, 1, 4, 3)
    ctt = jnp.tile(ct, (1, 1, 1, 1, t))
    n_ap = 2 * SSM_SCAN_STEPS + 2
    blk = lambda *s: pl.BlockSpec((1,) + s, lambda i: (i,) + (0,) * len(s))
    return pl.pallas_call(
        _ssm_prep_kernel,
        grid=(g,),
        in_specs=[blk(8, LANES), blk(p, 8), blk(2, 2, hg, p), blk(2, 2, t * hg, p), blk(2, 2, p, t * hg)],
        out_specs=[blk(SSM_TW, SSM_TW), blk(SSM_TW, 4 * p), blk(4 * p, SSM_TW), blk(2, n_ap, LANES)],
        out_shape=[jax.ShapeDtypeStruct((g, SSM_TW, SSM_TW), BF16),
                   jax.ShapeDtypeStruct((g, SSM_TW, 4 * p), BF16),
                   jax.ShapeDtypeStruct((g, 4 * p, SSM_TW), BF16),
                   jax.ShapeDtypeStruct((g, 2, n_ap, LANES), F32)],
        compiler_params=_cparams("arbitrary"),
        name="ssm_prep",
    )(prow, pcol, bt, btt, ctt)


def _atom_transpose_perm():
    i = jnp.arange(SSM_PERM_W)
    a, b, h = i // LANES, (i % LANES) // SSM_GROUP, i % SSM_GROUP
    j = b * LANES + a * SSM_GROUP + h
    return (j[:, None] == i[None, :]).astype(BF16)


def _ssm_kernel(u_lo_ref, u_hi_ref, perm_ref, m_ref, wst_ref, cp_ref, ap_ref, y_lo_ref, y_hi_ref,
                ug_ref, yg_ref, z_ref, s_ref):
    u_refs, y_refs = (u_lo_ref, u_hi_ref), (y_lo_ref, y_hi_ref)
    nc = ug_ref.shape[1]
    perm = perm_ref[...]
    blocks = [(tb, half) for tb in range(SSM_T // SSM_PER_TILE) for half in range(SSM_GROUPS // SSM_PER_TILE)]

    def token_rows(refs, tb, t, half):
        return refs[half].at[0, pl.ds(tb * SSM_PER_TILE + t, nc, stride=SSM_T), :]

    for tb, half in blocks:
        x = jnp.concatenate([token_rows(u_refs, tb, t, half)[...] for t in range(SSM_PER_TILE)], axis=1)
        r = _dot(x.astype(BF16), perm).astype(BF16)
        for gi in range(SSM_PER_TILE):
            ug_ref[half * SSM_PER_TILE + gi, :, tb * LANES:(tb + 1) * LANES] = r[:, gi * LANES:(gi + 1) * LANES]

    for g in range(SSM_GROUPS):
        z_ref[g] = _dot(ug_ref[g], wst_ref[g])

    rows = SSM_GROUPS * nc
    chunk = lax.broadcasted_iota(jnp.int32, (rows, LANES), 0) & (nc - 1)

    def shifted(s3, sh, d):
        flat = s3.reshape(rows, LANES)
        if d == 0:
            return jnp.where(chunk >= sh, pltpu.roll(flat, sh, 0), 0.0)
        return jnp.where(chunk < nc - sh, pltpu.roll(flat, rows - sh, 0), 0.0)

    as3d = lambda a: a.reshape(SSM_GROUPS, nc, LANES)
    for d in range(2):
        s3 = z_ref[:, :, d * LANES:(d + 1) * LANES]
        for k in range(nc.bit_length() - 1):
            t = shifted(s3, 1 << k, d)
            s3 = s3 + ap_ref[:, d, 2 * k:2 * k + 1, :] * as3d(t) \
                + ap_ref[:, d, 2 * k + 1:2 * k + 2, :] * as3d(pltpu.roll(t, SSM_STATE, 1))
        s_ref[:, :, d * LANES:(d + 1) * LANES] = as3d(shifted(s3, 1, d)).astype(BF16)

    for g in range(SSM_GROUPS):
        yg_ref[g] = _dot(ug_ref[g], m_ref[g]) + _dot(s_ref[g], cp_ref[g])

    for tb, half in blocks:
        yv = jnp.concatenate([yg_ref[half * SSM_PER_TILE + gi, :, tb * LANES:(tb + 1) * LANES]
                              for gi in range(SSM_PER_TILE)], axis=1)
        hi = yv.astype(BF16)
        lo = (yv - hi.astype(F32)).astype(BF16)
        out = _dot(hi, perm) + _dot(lo, perm)
        for t in range(SSM_PER_TILE):
            token_rows(y_refs, tb, t, half)[...] = out[:, t * LANES:(t + 1) * LANES]


def _ssm(u, tables):
    m, wst, cp, ap = tables
    b, l, w = u.shape
    nc = l // SSM_T
    assert nc & (nc - 1) == 0 and nc <= 1 << SSM_SCAN_STEPS, "chunk count must be a power of two"
    perm = _atom_transpose_perm()
    assert w == 2 * LANES
    half = lambda j: pl.BlockSpec((1, l, LANES), lambda i: (i, 0, j))
    return pl.pallas_call(
        _ssm_kernel,
        grid=(b,),
        in_specs=[half(0), half(1)] + [_const_spec(a.shape) for a in (perm, m, wst, cp, ap)],
        out_specs=[half(0), half(0)],
        out_shape=[jax.ShapeDtypeStruct((b, l, LANES), F32)] * 2,
        scratch_shapes=[pltpu.VMEM((SSM_GROUPS, nc, SSM_TW), BF16),
                        pltpu.VMEM((SSM_GROUPS, nc, SSM_TW), F32),
                        pltpu.VMEM((SSM_GROUPS, nc, 2 * LANES), F32),
                        pltpu.VMEM((SSM_GROUPS, nc, 2 * LANES), BF16)],
        compiler_params=_cparams("arbitrary"),
        name="ssm_scan",
    )(u, u, perm, m, wst, cp, ap)


def _gelu_tanh(x):
    return 0.5 * x * (1.0 + jnp.tanh(math.sqrt(2.0 / math.pi) * (x + 0.044715 * (x * x * x))))


def _merge_kernel(x_ref, oa_ref, y_lo_ref, y_hi_ref, u_ref, om_ref,
                  gmix_ref, wg_ref, d_ref, wglu_ref, wa_ref, ws_ref, wm_ref, wout_ref, o_ref):
    x = x_ref[...]
    h = _rms(x, gmix_ref[...], D_MODEL).astype(BF16)
    y = jnp.concatenate([y_lo_ref[...], y_hi_ref[...]], axis=1)
    y = _gelu_tanh(y + d_ref[...] * u_ref[...])
    o_ssm = (y * _sigmoid(_dot(y.astype(BF16), wglu_ref[...]))).astype(BF16)
    branches = ((oa_ref[...], wa_ref), (o_ssm, ws_ref), (om_ref[...], wm_ref))
    merged = None
    for i, (o_b, w_ref) in enumerate(branches):
        gate = _sigmoid(_dot(h, wg_ref[:, i * D_MODEL:(i + 1) * D_MODEL]))
        term = gate * _dot(o_b, w_ref[...])
        merged = term if merged is None else merged + term
    o_ref[...] = x + _dot(merged.astype(BF16), wout_ref[...])


def _merge(x2, oa2, y_lo, y_hi, u2, om2, wts):
    n = x2.shape[0]
    tm = TOKEN_TILE
    tile = lambda a: pl.BlockSpec((tm, a.shape[1]), lambda i: (i, 0))
    return pl.pallas_call(
        _merge_kernel,
        grid=(n // tm,),
        in_specs=[tile(x2), tile(oa2), tile(y_lo), tile(y_hi), tile(u2), tile(om2)]
                 + [_const_spec(w.shape) for w in wts],
        out_specs=tile(x2),
        out_shape=jax.ShapeDtypeStruct(x2.shape, F32),
        compiler_params=_cparams("arbitrary"),
        name="merge_out",
    )(x2, oa2, y_lo, y_hi, u2, om2, *wts)


def _ffn_kernel(x_ref, g_ref, wgate_ref, wup_ref, wdown_ref, o_ref):
    x = x_ref[...]
    h = _rms(x, g_ref[...], D_MODEL).astype(BF16)
    gate = _dot(h, wgate_ref[...])
    act = (gate * _sigmoid(gate) * _dot(h, wup_ref[...])).astype(BF16)
    o_ref[...] = x + _dot(act, wdown_ref[...])


def _ffn(x2, wts):
    n = x2.shape[0]
    tm = TOKEN_TILE
    tile = pl.BlockSpec((tm, D_MODEL), lambda i: (i, 0))
    return pl.pallas_call(
        _ffn_kernel,
        grid=(n // tm,),
        in_specs=[tile] + [_const_spec(w.shape) for w in wts],
        out_specs=tile,
        out_shape=jax.ShapeDtypeStruct(x2.shape, F32),
        compiler_params=_cparams("arbitrary"),
        name="ffn",
    )(x2, *wts)


def _pad_heads(w, heads, width):
    lead = w.shape[:-1]
    w = w.reshape(lead + (heads, width))
    w = jnp.pad(w, [(0, 0)] * len(lead) + [(0, 0), (0, LANES - width)])
    return w.reshape(lead + (heads * LANES,))


def _dup_first_rope_half(w):
    return jnp.concatenate([w[..., :QK_DIM], w[..., ROPE_LO:ROPE_HI],
                            w[..., QK_DIM + HALF_ROPE:]], axis=-1)


def _pad_row(g):
    return jnp.pad(g, (0, LANES - g.shape[0])).reshape(1, LANES)


def _rope_tables(length, gain, scale):
    inv = 1.0 / (ROPE_THETA ** (jnp.arange(0, QK_ROPE, 2, dtype=F32) / QK_ROPE))
    ang = jnp.arange(length, dtype=F32)[:, None] * inv[None, :]
    cos, sin = jnp.cos(ang), jnp.sin(ang)
    g = gain * scale
    g_nope, g_lo, g_hi = g[:QK_NOPE], g[ROPE_LO:ROPE_HI], g[ROPE_HI:QK_DIM]
    pad = jnp.zeros((length, LANES - QK_DIM), F32)
    a = jnp.concatenate([jnp.broadcast_to(g_nope, (length, QK_NOPE)), cos * g_lo, cos * g_hi, pad], axis=1)
    b = jnp.concatenate([jnp.zeros((length, QK_NOPE), F32), -sin * g_hi, sin * g_lo, pad], axis=1)
    return a, b


def _layer(x, mem, rope, score_bound, p):
    b, l, d = x.shape
    mkt, mv = _mem_kv(mem, p["g_mem"], p["w_mem_kv"], p["g_mk"])
    q, k, vt, u, o_mem = _in_proj(x, rope, mkt, mv, p["in_proj"])
    o_attn = _attention(score_bound, q, k, vt)

    y_lo, y_hi = (y.reshape(b * l, LANES) for y in _ssm(u, p["ssm"]))

    x2 = x.reshape(b * l, d)
    x1 = _merge(x2, o_attn.reshape(b * l, -1), y_lo, y_hi, u.reshape(b * l, SSM_W),
                o_mem.reshape(b * l, -1), p["merge"])
    return _ffn(x1, p["ffn"]).reshape(b, l, d)


def kernel(x_prompt, x_sample, mem_prompt, mem_sample, norm_mix_g, norm_mem_g, w_in, q_lora_norm_g, kv_lora_norm_g, w_uq, w_ukv, mla_q_norm_g, mla_k_norm_g, ssm_lambda_re, ssm_lambda_im, ssm_log_step, ssm_b_re, ssm_b_im, ssm_c_re, ssm_c_im, ssm_d, ssm_w_glu, w_mem_kv, mem_q_norm_g, mem_k_norm_g, w_br_attn, w_br_ssm, w_br_mem, w_out, norm_ffn_g, w_ffn_gate, w_ffn_up, w_ffn_down):
    assert norm_mix_g.shape[0] == 1, "single-layer encoder"
    row = lambda g: g.reshape(1, -1)
    w_in0 = w_in[0]

    k_rope_cols = jnp.pad(w_in0[:, OFF_KV:OFF_KR], ((0, 0), (QK_NOPE, LANES - QK_DIM)))
    w1 = jnp.concatenate([w_in0[:, :OFF_KV], _dup_first_rope_half(k_rope_cols),
                          w_in0[:, OFF_KR:OFF_SSM],
                          _pad_heads(w_in0[:, OFF_SSM:OFF_MEM], MEM_HEADS, MEM_HD)], axis=1).astype(BF16)
    w_uq_p = _pad_heads(w_uq[0], MLA_HEADS, QK_DIM).reshape(Q_LORA, MLA_HEADS, LANES)
    w_uq_p = _dup_first_rope_half(w_uq_p).reshape(Q_LORA, -1).astype(BF16)
    w_ukv_h = w_ukv[0].reshape(KV_LORA, MLA_HEADS, QK_NOPE + V_HEAD)
    w_k = _pad_heads(w_ukv_h[..., :QK_NOPE].reshape(KV_LORA, -1), MLA_HEADS, QK_NOPE)
    w_v = w_ukv_h[..., QK_NOPE:].reshape(KV_LORA, -1)
    w_ukv_p = jnp.concatenate([w_k, w_v], axis=1).astype(BF16)

    w_mkv = w_mem_kv[0]
    w_mv = _pad_heads(w_mkv[:, MEM_W:], MEM_HEADS, MEM_HD).reshape(D_MODEL, MEM_HEADS, LANES)
    w_mv = jnp.where((jnp.arange(MEM_HEADS) % 2 == 1)[None, :, None], jnp.roll(w_mv, MEM_HD, axis=-1), w_mv)
    w_mkv_p = jnp.concatenate([_pad_heads(w_mkv[:, :MEM_W], MEM_HEADS, MEM_HD),
                               w_mv.reshape(D_MODEL, -1)], axis=1).astype(BF16)

    params = {
        "g_mem": row(norm_mem_g[0]), "w_mem_kv": w_mkv_p, "g_mk": _pad_row(mem_k_norm_g[0]),
        "in_proj": (row(norm_mix_g[0]), w1, row(q_lora_norm_g[0]), w_uq_p, row(kv_lora_norm_g[0]), w_ukv_p,
                    _pad_row(mem_q_norm_g[0] * MEM_Q_SCALE)),
        "ssm": _ssm_prep(ssm_lambda_re[0], ssm_lambda_im[0], ssm_log_step[0],
                         ssm_b_re[0], ssm_b_im[0], ssm_c_re[0], ssm_c_im[0]),
        "merge": (row(norm_mix_g[0]), w_in0[:, OFF_MEM:].astype(BF16), row(ssm_d[0]),
                  ssm_w_glu[0].astype(BF16), w_br_attn[0].astype(BF16), w_br_ssm[0].astype(BF16),
                  w_br_mem[0].astype(BF16), w_out[0].astype(BF16)),
        "ffn": (row(norm_ffn_g[0]), w_ffn_gate[0].astype(BF16), w_ffn_up[0].astype(BF16),
                w_ffn_down[0].astype(BF16)),
    }
    score_bound = (1.02 * QK_DIM * Q_SCALE * jnp.max(jnp.abs(mla_q_norm_g[0]))
                   * jnp.max(jnp.abs(mla_k_norm_g[0]))).astype(F32)

    def rope(length):
        return (_rope_tables(length, mla_q_norm_g[0], Q_SCALE)
                + _rope_tables(length, mla_k_norm_g[0], 1.0))

    rope_p = rope(x_prompt.shape[1])
    rope_s = rope_p if x_sample.shape[1] == x_prompt.shape[1] else rope(x_sample.shape[1])
    y_prompt = _layer(x_prompt, mem_prompt, rope_p, score_bound, params)
    y_sample = _layer(x_sample, mem_sample, rope_s, score_bound, params)
    return (y_prompt, y_sample)
```

```python
import functools
import math

import jax
import jax.numpy as jnp
from jax import lax
from jax.experimental import pallas as pl
from jax.experimental.pallas import tpu as pltpu

F32 = jnp.float32
BF16 = jnp.bfloat16

D_MODEL = 1024
MLA_HEADS = 8
QK_NOPE = 64
QK_ROPE = 32
QK_DIM = QK_NOPE + QK_ROPE
V_HEAD = 64
Q_LORA = 384
KV_LORA = 256
ROPE_THETA = 10000.0
SSM_GROUP = 16
SSM_W = 256
SSM_GROUPS = SSM_W // SSM_GROUP
SSM_STATE = 64
MEM_HEADS = 4
MEM_HD = 64
MEM_W = MEM_HEADS * MEM_HD
N_BRANCH = 3
FF = -(-8 * D_MODEL // (3 * 256)) * 256
EPS = 1e-6
OFF_Q = Q_LORA
OFF_KV = OFF_Q + KV_LORA
OFF_KR = OFF_KV + QK_ROPE
OFF_SSM = OFF_KR + SSM_W
OFF_MEM = OFF_SSM + MEM_W

LANES = 128
HALF_ROPE = QK_ROPE // 2
ROPE_LO = QK_NOPE
ROPE_HI = QK_NOPE + HALF_ROPE
LOG2E = 1.4426950408889634
Q_SCALE = (QK_DIM ** -0.5) * LOG2E
MEM_Q_SCALE = (MEM_HD ** -0.5) * LOG2E
SAFE_SCORE_LOG2 = 60.0

Z_CQ = 0
Z_CKV = Z_CQ + Q_LORA
Z_KR = Z_CKV + KV_LORA
Z_U = Z_KR + LANES
Z_QM = Z_U + SSM_W
Z_END = Z_QM + MEM_HEADS * LANES

SSM_T = 32
SSM_TW = SSM_T * SSM_GROUP
SSM_SCAN_STEPS = 7
SSM_PER_TILE = LANES // SSM_GROUP
SSM_PERM_W = SSM_PER_TILE * LANES

VT_ROWS = V_HEAD + 16

TOKEN_TILE = 512
IN_PROJ_TILE = 1024
ATTN_Q_TILE = 512
VMEM_LIMIT = 52 * 1024 * 1024


def _cparams(*sem):
    return pltpu.CompilerParams(dimension_semantics=sem, vmem_limit_bytes=VMEM_LIMIT)


def _const_spec(shape):
    n = len(shape)
    return pl.BlockSpec(shape, lambda *_: (0,) * n, pipeline_mode=pl.Buffered(1))


def _rms(x, g, n):
    ms = jnp.sum(x * x, axis=-1, keepdims=True) * (1.0 / n)
    return x * lax.rsqrt(ms + EPS) * g


def _sigmoid(x):
    return 1.0 / (1.0 + jnp.exp(-x))


def _dot(a, b):
    return jnp.dot(a, b, preferred_element_type=F32)


def _lane_onehot(lane_idx):
    return jnp.where(lax.broadcasted_iota(jnp.int32, (1, LANES), 1) == lane_idx, 1.0, 0.0)


def _mem_ones_lane(head):
    return MEM_HD if head % 2 == 0 else 0


def _head_rsqrt_mxu(t, n_valid):
    ones = jnp.where(lax.broadcasted_iota(jnp.int32, (LANES, LANES), 0) < n_valid, 1.0, 0.0)
    ss = _dot((t * t).astype(BF16), ones.astype(BF16))
    return lax.rsqrt(ss * (1.0 / n_valid) + EPS)


def _swap_rope(t):
    return pltpu.roll(t, LANES - HALF_ROPE, 1)


def _mem_kv_kernel(mem_ref, g_ref, w_ref, gk_ref, mkt_ref, mv_ref):
    m = _rms(mem_ref[0], g_ref[...], D_MODEL).astype(BF16)
    kv = _dot(m, w_ref[...])
    for h in range(MEM_HEADS):
        k = _rms(kv[:, h * LANES:(h + 1) * LANES], gk_ref[...], MEM_HD)
        mkt_ref[0, h] = k.T.astype(BF16)
        off = (MEM_HEADS + h) * LANES
        mv_ref[0, h] = (kv[:, off:off + LANES] + _lane_onehot(_mem_ones_lane(h))).astype(BF16)


def _mem_kv(mem, g_mem, w_kv, g_k):
    b, m, _ = mem.shape
    return pl.pallas_call(
        _mem_kv_kernel,
        grid=(b,),
        in_specs=[pl.BlockSpec((1, m, D_MODEL), lambda i: (i, 0, 0)),
                  _const_spec((1, D_MODEL)),
                  _const_spec((D_MODEL, 2 * MEM_HEADS * LANES)),
                  _const_spec((1, LANES))],
        out_specs=[pl.BlockSpec((1, MEM_HEADS, LANES, m), lambda i: (i, 0, 0, 0)),
                   pl.BlockSpec((1, MEM_HEADS, m, LANES), lambda i: (i, 0, 0, 0))],
        out_shape=[jax.ShapeDtypeStruct((b, MEM_HEADS, LANES, m), BF16),
                   jax.ShapeDtypeStruct((b, MEM_HEADS, m, LANES), BF16)],
        compiler_params=_cparams("arbitrary"),
        name="mem_kv",
    )(mem, g_mem, w_kv, g_k)


def _in_proj_kernel(x_ref, aq_ref, bq_ref, ak_ref, bk_ref, mkt_ref, mv_ref,
                    gmix_ref, w1_ref, gq_ref, wuq_ref, gkv_ref, wukv_ref, gmq_ref,
                    q_ref, k_ref, vt_ref, u_ref, om_ref):
    h = _rms(x_ref[0], gmix_ref[...], D_MODEL).astype(BF16)
    z = _dot(h, w1_ref[...])
    u_ref[0] = z[:, Z_U:Z_U + SSM_W]

    low_half = lax.broadcasted_iota(jnp.int32, (1, LANES), 1) < MEM_HD

    def memory_head(hd):
        off = Z_QM + hd * LANES
        mq = z[:, off:off + LANES]
        r = lax.rsqrt(jnp.sum(mq * mq, axis=-1, keepdims=True) * (1.0 / MEM_HD) + EPS)
        s = _dot((mq * gmq_ref[...] * r).astype(BF16), mkt_ref[0, hd])
        p = jnp.exp2(s - jnp.max(s, axis=-1, keepdims=True))
        o = _dot(p.astype(BF16), mv_ref[0, hd])
        one = _mem_ones_lane(hd)
        return o * (1.0 / o[:, one:one + 1])

    def memory_head_pair(pair):
        even, odd = memory_head(2 * pair), memory_head(2 * pair + 1)
        om_ref[0, :, pair * LANES:(pair + 1) * LANES] = jnp.where(low_half, even, odd).astype(BF16)

    cq = _rms(z[:, Z_CQ:Z_CQ + Q_LORA], gq_ref[...], Q_LORA).astype(BF16)
    q_all = _dot(cq, wuq_ref[...])
    ckv = _rms(z[:, Z_CKV:Z_CKV + KV_LORA], gkv_ref[...], KV_LORA).astype(BF16)
    kv = _dot(ckv, wukv_ref[...])

    aq, bq, ak, bk = aq_ref[...], bq_ref[...], ak_ref[...], bk_ref[...]
    k_rope = z[:, Z_KR:Z_KR + LANES]
    k_rope_rot = k_rope * ak + _swap_rope(k_rope) * bk
    real_lane = lax.broadcasted_iota(jnp.int32, (1, LANES), 1) < QK_DIM
    ss_rope = jnp.sum(jnp.where(real_lane, k_rope * k_rope, 0.0), axis=-1, keepdims=True)
    hw = MLA_HEADS * LANES

    for hd in range(MLA_HEADS):
        sl = slice(hd * LANES, (hd + 1) * LANES)
        qh = q_all[:, sl]
        r = _head_rsqrt_mxu(qh, QK_DIM)
        q_ref[0, hd] = ((qh * aq + _swap_rope(qh) * bq) * r).astype(BF16)
        kh = kv[:, sl]
        ss = jnp.sum(kh * kh, axis=-1, keepdims=True) + ss_rope
        r = lax.rsqrt(ss * (1.0 / QK_DIM) + EPS)
        k_ref[0, hd] = ((kh * ak + k_rope_rot) * r).astype(BF16)
        heads_per_pair = 2 * MLA_HEADS // MEM_HEADS
        if hd % heads_per_pair == 0:
            memory_head_pair(hd // heads_per_pair)

    v_t = kv[:, hw:hw + MLA_HEADS * V_HEAD].T.astype(BF16)
    tm = v_t.shape[1]
    ones_rows = jnp.where(lax.broadcasted_iota(jnp.int32, (VT_ROWS - V_HEAD, tm), 0) == 0, 1.0, 0.0)
    for hd in range(MLA_HEADS):
        vt_ref[0, hd, :V_HEAD, :] = v_t[hd * V_HEAD:(hd + 1) * V_HEAD, :]
        vt_ref[0, hd, V_HEAD:, :] = ones_rows.astype(BF16)


def _in_proj(x, rope, mkt, mv, wts):
    b, l, _ = x.shape
    tm = min(IN_PROJ_TILE, l)
    m = mkt.shape[-1]
    tile = lambda last: pl.BlockSpec((1, tm, last), lambda i, j: (i, j, 0))
    head_tile = pl.BlockSpec((1, MLA_HEADS, tm, LANES), lambda i, j: (i, 0, j, 0))
    rope_spec = pl.BlockSpec((tm, LANES), lambda i, j: (j, 0))
    return pl.pallas_call(
        _in_proj_kernel,
        grid=(b, l // tm),
        in_specs=[tile(D_MODEL), rope_spec, rope_spec, rope_spec, rope_spec,
                  pl.BlockSpec((1, MEM_HEADS, LANES, m), lambda i, j: (i, 0, 0, 0)),
                  pl.BlockSpec((1, MEM_HEADS, m, LANES), lambda i, j: (i, 0, 0, 0))]
                 + [_const_spec(w.shape) for w in wts],
        out_specs=[head_tile, head_tile,
                   pl.BlockSpec((1, MLA_HEADS, VT_ROWS, tm), lambda i, j: (i, 0, 0, j)),
                   tile(SSM_W), tile(MEM_W)],
        out_shape=[jax.ShapeDtypeStruct((b, MLA_HEADS, l, LANES), BF16),
                   jax.ShapeDtypeStruct((b, MLA_HEADS, l, LANES), BF16),
                   jax.ShapeDtypeStruct((b, MLA_HEADS, VT_ROWS, l), BF16),
                   jax.ShapeDtypeStruct((b, l, SSM_W), F32),
                   jax.ShapeDtypeStruct((b, l, MEM_W), BF16)],
        compiler_params=_cparams("arbitrary", "arbitrary"),
        name="in_proj",
    )(x, *rope, mkt, mv, *wts)


def _attn_kernel(q_ref, k_ref, vt_ref, o_ref, *, subtract_max):
    outs = []
    for j in range(2):
        s_t = lax.dot_general(k_ref[0, j], q_ref[0, j], (((1,), (1,)), ((), ())),
                              preferred_element_type=F32)
        if subtract_max:
            s_t = s_t - jnp.max(s_t, axis=0, keepdims=True)
        o_t = _dot(vt_ref[0, j], jnp.exp2(s_t).astype(BF16))
        outs.append(o_t[:V_HEAD] * (1.0 / o_t[V_HEAD:V_HEAD + 1]))
    o_ref[0] = jnp.concatenate(outs, axis=0).T.astype(BF16)


def _attention(score_bound, q, k, vt):
    b, _, l, _ = q.shape
    tq = ATTN_Q_TILE

    def call(subtract_max):
        return pl.pallas_call(
            functools.partial(_attn_kernel, subtract_max=subtract_max),
            grid=(b, MLA_HEADS // 2, l // tq),
            in_specs=[pl.BlockSpec((1, 2, tq, LANES), lambda i, h, j: (i, h, j, 0)),
                      pl.BlockSpec((1, 2, l, LANES), lambda i, h, j: (i, h, 0, 0)),
                      pl.BlockSpec((1, 2, VT_ROWS, l), lambda i, h, j: (i, h, 0, 0))],
            out_specs=pl.BlockSpec((1, tq, LANES), lambda i, h, j: (i, j, h)),
            out_shape=jax.ShapeDtypeStruct((b, l, MLA_HEADS * V_HEAD), BF16),
            compiler_params=_cparams("arbitrary", "arbitrary", "arbitrary"),
            name="mla_attn_shifted" if subtract_max else "mla_attn",
        )

    return lax.cond(score_bound <= SAFE_SCORE_LOG2, call(False), call(True), q, k, vt)


def _cpow(r, th, e):
    mag = jnp.exp(r * e)
    return mag * jnp.cos(th * e), mag * jnp.sin(th * e)


def _zoh(lr, li, dt):
    a_re, a_im = _cpow(lr * dt, li * dt, 1.0)
    den = lr * lr + li * li
    n_re = a_re - 1.0
    return (n_re * lr + a_im * li) / den, (a_im * lr - n_re * li) / den


def _cmul(ar, ai, br, bi):
    return ar * br - ai * bi, ar * bi + ai * br


def _ssm_prep_kernel(prow_ref, pcol_ref, bt_ref, btt_ref, ctt_ref,
                     m_ref, wst_ref, cp_ref, ap_ref):
    iota = lax.broadcasted_iota
    t_row = jnp.minimum(iota(jnp.int32, (LANES, SSM_STATE), 0), SSM_T - 1).astype(F32)
    t_lane = jnp.minimum(iota(jnp.int32, (SSM_STATE, LANES), 1), SSM_T - 1).astype(F32)
    spread_rows = jnp.where(iota(jnp.int32, (SSM_TW, LANES), 0) // SSM_GROUP
                            == iota(jnp.int32, (SSM_TW, LANES), 1), 1.0, 0.0)
    spread_lanes = jnp.where(iota(jnp.int32, (LANES, SSM_TW), 1) // SSM_GROUP
                             == iota(jnp.int32, (LANES, SSM_TW), 0), 1.0, 0.0)
    exact_dot = lambda a, b: jnp.dot(a, b, preferred_element_type=F32, precision=lax.Precision.HIGHEST)

    def pow_rows(r, th, e):
        p_re, p_im = _cpow(r, th, e)
        return exact_dot(spread_rows, p_re), exact_dot(spread_rows, p_im)

    def pow_lanes(r, th, e):
        p_re, p_im = _cpow(r, th, e)
        return exact_dot(p_re, spread_lanes), exact_dot(p_im, spread_lanes)

    lane = lax.broadcasted_iota(jnp.int32, (SSM_GROUP, SSM_TW), 1)
    k_row = lax.broadcasted_iota(jnp.int32, (2 * SSM_SCAN_STEPS + 2, LANES), 0)
    chunk_pow = (SSM_T * jnp.left_shift(1, k_row // 2)).astype(F32)
    odd_row = (k_row % 2) == 1
    lane_ap = lax.broadcasted_iota(jnp.int32, (2 * SSM_SCAN_STEPS + 2, LANES), 1)

    toeplitz = []
    for d in range(2):
        lr2, li2 = prow_ref[0, 3 * d:3 * d + 1, :], prow_ref[0, 3 * d + 1:3 * d + 2, :]
        dt2 = jnp.exp(prow_ref[0, 3 * d + 2:3 * d + 3, :])
        lr, li, dt = lr2[:, :SSM_STATE], li2[:, :SSM_STATE], dt2[:, :SSM_STATE]
        z_re, z_im = _zoh(lr, li, dt)

        p_re, p_im = _cpow(lr2 * dt2, li2 * dt2, chunk_pow)
        ap_ref[0, d] = jnp.where(odd_row, jnp.where(lane_ap < SSM_STATE, -p_im, p_im), p_re)

        e_in = (SSM_T - 1.0) - t_row if d == 0 else t_row
        w_re, w_im = _cmul(*pow_rows(lr * dt, li * dt, e_in), z_re, z_im)
        b_re, b_im = btt_ref[0, d, 0], btt_ref[0, d, 1]
        s_re, s_im = _cmul(w_re, w_im, b_re, b_im)
        wst_ref[0, :, 2 * d * SSM_STATE:(2 * d + 1) * SSM_STATE] = s_re.astype(BF16)
        wst_ref[0, :, (2 * d + 1) * SSM_STATE:(2 * d + 2) * SSM_STATE] = s_im.astype(BF16)

        lrc, lic = pcol_ref[0, :, 3 * d:3 * d + 1], pcol_ref[0, :, 3 * d + 1:3 * d + 2]
        dtc = jnp.exp(pcol_ref[0, :, 3 * d + 2:3 * d + 3])
        c_re, c_im = ctt_ref[0, d, 0], ctt_ref[0, d, 1]

        e_out = t_lane + 1.0 if d == 0 else SSM_T - t_lane
        g_re, g_im = _cmul(*pow_lanes(lrc * dtc, lic * dtc, e_out), c_re, c_im)
        cp_ref[0, 2 * d * SSM_STATE:(2 * d + 1) * SSM_STATE, :] = g_re.astype(BF16)
        cp_ref[0, (2 * d + 1) * SSM_STATE:(2 * d + 2) * SSM_STATE, :] = (-g_im).astype(BF16)

        e_lag = t_lane if d == 0 else (SSM_T - 1.0) - t_lane
        k_re, k_im = _cmul(*pow_lanes(lrc * dtc, lic * dtc, e_lag), c_re, c_im)
        bb_re, bb_im = _cmul(z_re, z_im, bt_ref[0, d, 0], bt_ref[0, d, 1])
        lag = (jnp.dot(bb_re, k_re, preferred_element_type=F32, precision=lax.Precision.HIGHEST)
               - jnp.dot(bb_im, k_im, preferred_element_type=F32, precision=lax.Precision.HIGHEST))
        toeplitz.append(lag)

    lag_f, lag_b = toeplitz
    for s in range(SSM_T):
        fwd = lag_f if s == 0 else pltpu.roll(lag_f, SSM_GROUP * s, 1)
        fwd = jnp.where(lane >= SSM_GROUP * s, fwd, 0.0)
        shift = (SSM_TW - SSM_GROUP * (SSM_T - 1 - s)) % SSM_TW
        bwd = lag_b if shift == 0 else pltpu.roll(lag_b, shift, 1)
        bwd = jnp.where(lane < SSM_GROUP * (s + 1), bwd, 0.0)
        m_ref[0, s * SSM_GROUP:(s + 1) * SSM_GROUP, :] = (fwd + bwd).astype(BF16)


def _ssm_prep(lam_re, lam_im, log_step, b_re, b_im, c_re, c_im):
    g, p, hg, t = SSM_GROUPS, SSM_STATE, SSM_GROUP, SSM_T
    step = jnp.broadcast_to(log_step[:, :, None], (2, g, p))
    rows = jnp.stack([lam_re[0], lam_im[0], step[0], lam_re[1], lam_im[1], step[1]], axis=1)
    rows = jnp.pad(rows, ((0, 0), (0, 2), (0, 0)))
    prow = jnp.concatenate([rows, rows], axis=-1)
    pcol = rows.transpose(0, 2, 1)
    bt = jnp.stack([b_re, b_im], axis=1).transpose(2, 0, 1, 4, 3)
    btt = jnp.tile(bt, (1, 1, 1, t, 1))
    ct = jnp.stack([c_re, c_im], axis=1).transpose(2, 0, 1, 4, 3)
    ctt = jnp.tile(ct, (1, 1, 1, 1, t))
    n_ap = 2 * SSM_SCAN_STEPS + 2
    blk = lambda *s: pl.BlockSpec((1,) + s, lambda i: (i,) + (0,) * len(s))
    return pl.pallas_call(
        _ssm_prep_kernel,
        grid=(g,),
        in_specs=[blk(8, LANES), blk(p, 8), blk(2, 2, hg, p), blk(2, 2, t * hg, p), blk(2, 2, p, t * hg)],
        out_specs=[blk(SSM_TW, SSM_TW), blk(SSM_TW, 4 * p), blk(4 * p, SSM_TW), blk(2, n_ap, LANES)],
        out_shape=[jax.ShapeDtypeStruct((g, SSM_TW, SSM_TW), BF16),
                   jax.ShapeDtypeStruct((g, SSM_TW, 4 * p), BF16),
                   jax.ShapeDtypeStruct((g, 4 * p, SSM_TW), BF16),
                   jax.ShapeDtypeStruct((g, 2, n_ap, LANES), F32)],
        compiler_params=_cparams("arbitrary"),
        name="ssm_prep",
    )(prow, pcol, bt, btt, ctt)


def _atom_transpose_perm():
    i = jnp.arange(SSM_PERM_W)
    a, b, h = i // LANES, (i % LANES) // SSM_GROUP, i % SSM_GROUP
    j = b * LANES + a * SSM_GROUP + h
    return (j[:, None] == i[None, :]).astype(BF16)


def _ssm_kernel(u_lo_ref, u_hi_ref, perm_ref, m_ref, wst_ref, cp_ref, ap_ref, y_lo_ref, y_hi_ref,
                ug_ref, yg_ref, z_ref, s_ref):
    u_refs, y_refs = (u_lo_ref, u_hi_ref), (y_lo_ref, y_hi_ref)
    nc = ug_ref.shape[1]
    perm = perm_ref[...]
    blocks = [(tb, half) for tb in range(SSM_T // SSM_PER_TILE) for half in range(SSM_GROUPS // SSM_PER_TILE)]

    def token_rows(refs, tb, t, half):
        return refs[half].at[0, pl.ds(tb * SSM_PER_TILE + t, nc, stride=SSM_T), :]

    for tb, half in blocks:
        x = jnp.concatenate([token_rows(u_refs, tb, t, half)[...] for t in range(SSM_PER_TILE)], axis=1)
        r = _dot(x.astype(BF16), perm).astype(BF16)
        for gi in range(SSM_PER_TILE):
            ug_ref[half * SSM_PER_TILE + gi, :, tb * LANES:(tb + 1) * LANES] = r[:, gi * LANES:(gi + 1) * LANES]

    for g in range(SSM_GROUPS):
        z_ref[g] = _dot(ug_ref[g], wst_ref[g])

    rows = SSM_GROUPS * nc
    chunk = lax.broadcasted_iota(jnp.int32, (rows, LANES), 0) & (nc - 1)

    def shifted(s3, sh, d):
        flat = s3.reshape(rows, LANES)
        if d == 0:
            return jnp.where(chunk >= sh, pltpu.roll(flat, sh, 0), 0.0)
        return jnp.where(chunk < nc - sh, pltpu.roll(flat, rows - sh, 0), 0.0)

    as3d = lambda a: a.reshape(SSM_GROUPS, nc, LANES)
    for d in range(2):
        s3 = z_ref[:, :, d * LANES:(d + 1) * LANES]
        for k in range(nc.bit_length() - 1):
            t = shifted(s3, 1 << k, d)
            s3 = s3 + ap_ref[:, d, 2 * k:2 * k + 1, :] * as3d(t) \
                + ap_ref[:, d, 2 * k + 1:2 * k + 2, :] * as3d(pltpu.roll(t, SSM_STATE, 1))
        s_ref[:, :, d * LANES:(d + 1) * LANES] = as3d(shifted(s3, 1, d)).astype(BF16)

    for g in range(SSM_GROUPS):
        yg_ref[g] = _dot(ug_ref[g], m_ref[g]) + _dot(s_ref[g], cp_ref[g])

    for tb, half in blocks:
        yv = jnp.concatenate([yg_ref[half * SSM_PER_TILE + gi, :, tb * LANES:(tb + 1) * LANES]
                              for gi in range(SSM_PER_TILE)], axis=1)
        hi = yv.astype(BF16)
        lo = (yv - hi.astype(F32)).astype(BF16)
        out = _dot(hi, perm) + _dot(lo, perm)
        for t in range(SSM_PER_TILE):
            token_rows(y_refs, tb, t, half)[...] = out[:, t * LANES:(t + 1) * LANES]


def _ssm(u, tables):
    m, wst, cp, ap = tables
    b, l, w = u.shape
    nc = l // SSM_T
    assert nc & (nc - 1) == 0 and nc <= 1 << SSM_SCAN_STEPS, "chunk count must be a power of two"
    perm = _atom_transpose_perm()
    assert w == 2 * LANES
    half = lambda j: pl.BlockSpec((1, l, LANES), lambda i: (i, 0, j))
    return pl.pallas_call(
        _ssm_kernel,
        grid=(b,),
        in_specs=[half(0), half(1)] + [_const_spec(a.shape) for a in (perm, m, wst, cp, ap)],
        out_specs=[half(0), half(0)],
        out_shape=[jax.ShapeDtypeStruct((b, l, LANES), F32)] * 2,
        scratch_shapes=[pltpu.VMEM((SSM_GROUPS, nc, SSM_TW), BF16),
                        pltpu.VMEM((SSM_GROUPS, nc, SSM_TW), F32),
                        pltpu.VMEM((SSM_GROUPS, nc, 2 * LANES), F32),
                        pltpu.VMEM((SSM_GROUPS, nc, 2 * LANES), BF16)],
        compiler_params=_cparams("arbitrary"),
        name="ssm_scan",
    )(u, u, perm, m, wst, cp, ap)


def _gelu_tanh(x):
    return 0.5 * x * (1.0 + jnp.tanh(math.sqrt(2.0 / math.pi) * (x + 0.044715 * (x * x * x))))


def _merge_kernel(x_ref, oa_ref, y_lo_ref, y_hi_ref, u_ref, om_ref,
                  gmix_ref, wg_ref, d_ref, wglu_ref, wa_ref, ws_ref, wm_ref, wout_ref, o_ref):
    x = x_ref[...]
    h = _rms(x, gmix_ref[...], D_MODEL).astype(BF16)
    y = jnp.concatenate([y_lo_ref[...], y_hi_ref[...]], axis=1)
    y = _gelu_tanh(y + d_ref[...] * u_ref[...])
    o_ssm = (y * _sigmoid(_dot(y.astype(BF16), wglu_ref[...]))).astype(BF16)
    branches = ((oa_ref[...], wa_ref), (o_ssm, ws_ref), (om_ref[...], wm_ref))
    merged = None
    for i, (o_b, w_ref) in enumerate(branches):
        gate = _sigmoid(_dot(h, wg_ref[:, i * D_MODEL:(i + 1) * D_MODEL]))
        term = gate * _dot(o_b, w_ref[...])
        merged = term if merged is None else merged + term
    o_ref[...] = x + _dot(merged.astype(BF16), wout_ref[...])


def _merge(x2, oa2, y_lo, y_hi, u2, om2, wts):
    n = x2.shape[0]
    tm = min(2 * TOKEN_TILE, n)
    tile = lambda a: pl.BlockSpec((tm, a.shape[1]), lambda i: (i, 0))
    return pl.pallas_call(
        _merge_kernel,
        grid=(n // tm,),
        in_specs=[tile(x2), tile(oa2), tile(y_lo), tile(y_hi), tile(u2), tile(om2)]
                 + [_const_spec(w.shape) for w in wts],
        out_specs=tile(x2),
        out_shape=jax.ShapeDtypeStruct(x2.shape, F32),
        compiler_params=_cparams("arbitrary"),
        name="merge_out",
    )(x2, oa2, y_lo, y_hi, u2, om2, *wts)


def _ffn_kernel(x_ref, g_ref, wgate_ref, wup_ref, wdown_ref, o_ref):
    x = x_ref[...]
    h = _rms(x, g_ref[...], D_MODEL).astype(BF16)
    gate = _dot(h, wgate_ref[...])
    act = (gate * _sigmoid(gate) * _dot(h, wup_ref[...])).astype(BF16)
    o_ref[...] = x + _dot(act, wdown_ref[...])


def _ffn(x2, wts):
    n = x2.shape[0]
    tm = TOKEN_TILE
    tile = pl.BlockSpec((tm, D_MODEL), lambda i: (i, 0))
    return pl.pallas_call(
        _ffn_kernel,
        grid=(n // tm,),
        in_specs=[tile] + [_const_spec(w.shape) for w in wts],
        out_specs=tile,
        out_shape=jax.ShapeDtypeStruct(x2.shape, F32),
        compiler_params=_cparams("arbitrary"),
        name="ffn",
    )(x2, *wts)


def _pad_heads(w, heads, width):
    lead = w.shape[:-1]
    w = w.reshape(lead + (heads, width))
    w = jnp.pad(w, [(0, 0)] * len(lead) + [(0, 0), (0, LANES - width)])
    return w.reshape(lead + (heads * LANES,))


def _dup_first_rope_half(w):
    return jnp.concatenate([w[..., :QK_DIM], w[..., ROPE_LO:ROPE_HI],
                            w[..., QK_DIM + HALF_ROPE:]], axis=-1)


def _pad_row(g):
    return jnp.pad(g, (0, LANES - g.shape[0])).reshape(1, LANES)


def _rope_tables(length, gain, scale):
    inv = 1.0 / (ROPE_THETA ** (jnp.arange(0, QK_ROPE, 2, dtype=F32) / QK_ROPE))
    ang = jnp.arange(length, dtype=F32)[:, None] * inv[None, :]
    cos, sin = jnp.cos(ang), jnp.sin(ang)
    g = gain * scale
    g_nope, g_lo, g_hi = g[:QK_NOPE], g[ROPE_LO:ROPE_HI], g[ROPE_HI:QK_DIM]
    pad = jnp.zeros((length, LANES - QK_DIM), F32)
    a = jnp.concatenate([jnp.broadcast_to(g_nope, (length, QK_NOPE)), cos * g_lo, cos * g_hi, pad], axis=1)
    b = jnp.concatenate([jnp.zeros((length, QK_NOPE), F32), -sin * g_hi, sin * g_lo, pad], axis=1)
    return a, b


def _layer(x, mem, rope, score_bound, p):
    b, l, d = x.shape
    mkt, mv = _mem_kv(mem, p["g_mem"], p["w_mem_kv"], p["g_mk"])
    q, k, vt, u, o_mem = _in_proj(x, rope, mkt, mv, p["in_proj"])
    o_attn = _attention(score_bound, q, k, vt)

    y_lo, y_hi = (y.reshape(b * l, LANES) for y in _ssm(u, p["ssm"]))

    x2 = x.reshape(b * l, d)
    x1 = _merge(x2, o_attn.reshape(b * l, -1), y_lo, y_hi, u.reshape(b * l, SSM_W),
                o_mem.reshape(b * l, -1), p["merge"])
    return _ffn(x1, p["ffn"]).reshape(b, l, d)


def kernel(x_prompt, x_sample, mem_prompt, mem_sample, norm_mix_g, norm_mem_g, w_in, q_lora_norm_g, kv_lora_norm_g, w_uq, w_ukv, mla_q_norm_g, mla_k_norm_g, ssm_lambda_re, ssm_lambda_im, ssm_log_step, ssm_b_re, ssm_b_im, ssm_c_re, ssm_c_im, ssm_d, ssm_w_glu, w_mem_kv, mem_q_norm_g, mem_k_norm_g, w_br_attn, w_br_ssm, w_br_mem, w_out, norm_ffn_g, w_ffn_gate, w_ffn_up, w_ffn_down):
    assert norm_mix_g.shape[0] == 1, "single-layer encoder"
    row = lambda g: g.reshape(1, -1)
    w_in0 = w_in[0]

    k_rope_cols = jnp.pad(w_in0[:, OFF_KV:OFF_KR], ((0, 0), (QK_NOPE, LANES - QK_DIM)))
    w1 = jnp.concatenate([w_in0[:, :OFF_KV], _dup_first_rope_half(k_rope_cols),
                          w_in0[:, OFF_KR:OFF_SSM],
                          _pad_heads(w_in0[:, OFF_SSM:OFF_MEM], MEM_HEADS, MEM_HD)], axis=1).astype(BF16)
    w_uq_p = _pad_heads(w_uq[0], MLA_HEADS, QK_DIM).reshape(Q_LORA, MLA_HEADS, LANES)
    w_uq_p = _dup_first_rope_half(w_uq_p).reshape(Q_LORA, -1).astype(BF16)
    w_ukv_h = w_ukv[0].reshape(KV_LORA, MLA_HEADS, QK_NOPE + V_HEAD)
    w_k = _pad_heads(w_ukv_h[..., :QK_NOPE].reshape(KV_LORA, -1), MLA_HEADS, QK_NOPE)
    w_v = w_ukv_h[..., QK_NOPE:].reshape(KV_LORA, -1)
    w_ukv_p = jnp.concatenate([w_k, w_v], axis=1).astype(BF16)

    w_mkv = w_mem_kv[0]
    w_mv = _pad_heads(w_mkv[:, MEM_W:], MEM_HEADS, MEM_HD).reshape(D_MODEL, MEM_HEADS, LANES)
    w_mv = jnp.where((jnp.arange(MEM_HEADS) % 2 == 1)[None, :, None], jnp.roll(w_mv, MEM_HD, axis=-1), w_mv)
    w_mkv_p = jnp.concatenate([_pad_heads(w_mkv[:, :MEM_W], MEM_HEADS, MEM_HD),
                               w_mv.reshape(D_MODEL, -1)], axis=1).astype(BF16)

    params = {
        "g_mem": row(norm_mem_g[0]), "w_mem_kv": w_mkv_p, "g_mk": _pad_row(mem_k_norm_g[0]),
        "in_proj": (row(norm_mix_g[0]), w1, row(q_lora_norm_g[0]), w_uq_p, row(kv_lora_norm_g[0]), w_ukv_p,
                    _pad_row(mem_q_norm_g[0] * MEM_Q_SCALE)),
        "ssm": _ssm_prep(ssm_lambda_re[0], ssm_lambda_im[0], ssm_log_step[0],
                         ssm_b_re[0], ssm_b_im[0], ssm_c_re[0], ssm_c_im[0]),
        "merge": (row(norm_mix_g[0]), w_in0[:, OFF_MEM:].astype(BF16), row(ssm_d[0]),
                  ssm_w_glu[0].astype(BF16), w_br_attn[0].astype(BF16), w_br_ssm[0].astype(BF16),
                  w_br_mem[0].astype(BF16), w_out[0].astype(BF16)),
        "ffn": (row(norm_ffn_g[0]), w_ffn_gate[0].astype(BF16), w_ffn_up[0].astype(BF16),
                w_ffn_down[0].astype(BF16)),
    }
    score_bound = (1.02 * QK_DIM * Q_SCALE * jnp.max(jnp.abs(mla_q_norm_g[0]))
                   * jnp.max(jnp.abs(mla_k_norm_g[0]))).astype(F32)

    def rope(length):
        return (_rope_tables(length, mla_q_norm_g[0], Q_SCALE)
                + _rope_tables(length, mla_k_norm_g[0], 1.0))

    rope_p = rope(x_prompt.shape[1])
    rope_s = rope_p if x_sample.shape[1] == x_prompt.shape[1] else rope(x_sample.shape[1])
    y_prompt = _layer(x_prompt, mem_prompt, rope_p, score_bound, params)
    y_sample = _layer(x_sample, mem_sample, rope_s, score_bound, params)
    return (y_prompt, y_sample)
```

```python
import functools
import math

import jax
import jax.numpy as jnp
from jax import lax
from jax.experimental import pallas as pl
from jax.experimental.pallas import tpu as pltpu

F32 = jnp.float32
BF16 = jnp.bfloat16

D_MODEL = 1024
MLA_HEADS = 8
QK_NOPE = 64
QK_ROPE = 32
QK_DIM = QK_NOPE + QK_ROPE
V_HEAD = 64
Q_LORA = 384
KV_LORA = 256
ROPE_THETA = 10000.0
SSM_GROUP = 16
SSM_W = 256
SSM_GROUPS = SSM_W // SSM_GROUP
SSM_STATE = 64
MEM_HEADS = 4
MEM_HD = 64
MEM_W = MEM_HEADS * MEM_HD
N_BRANCH = 3
FF = -(-8 * D_MODEL // (3 * 256)) * 256
EPS = 1e-6
OFF_Q = Q_LORA
OFF_KV = OFF_Q + KV_LORA
OFF_KR = OFF_KV + QK_ROPE
OFF_SSM = OFF_KR + SSM_W
OFF_MEM = OFF_SSM + MEM_W

LANES = 128
HALF_ROPE = QK_ROPE // 2
ROPE_LO = QK_NOPE
ROPE_HI = QK_NOPE + HALF_ROPE
LOG2E = 1.4426950408889634
Q_SCALE = (QK_DIM ** -0.5) * LOG2E
MEM_Q_SCALE = (MEM_HD ** -0.5) * LOG2E
SAFE_SCORE_LOG2 = 60.0

Z_CQ = 0
Z_CKV = Z_CQ + Q_LORA
Z_KR = Z_CKV + KV_LORA
Z_U = Z_KR + LANES
Z_QM = Z_U + SSM_W
Z_END = Z_QM + MEM_HEADS * LANES

SSM_T = 32
SSM_TW = SSM_T * SSM_GROUP
SSM_SCAN_STEPS = 7
SSM_PER_TILE = LANES // SSM_GROUP
SSM_PERM_W = SSM_PER_TILE * LANES

VT_ROWS = V_HEAD + 16

TOKEN_TILE = 512
IN_PROJ_TILE = 1024
ATTN_Q_TILE = 2048
VMEM_LIMIT = 52 * 1024 * 1024


def _cparams(*sem):
    return pltpu.CompilerParams(dimension_semantics=sem, vmem_limit_bytes=VMEM_LIMIT)


def _const_spec(shape):
    n = len(shape)
    return pl.BlockSpec(shape, lambda *_: (0,) * n, pipeline_mode=pl.Buffered(1))


def _rms(x, g, n):
    ms = jnp.sum(x * x, axis=-1, keepdims=True) * (1.0 / n)
    return x * lax.rsqrt(ms + EPS) * g


def _sigmoid(x):
    return 1.0 / (1.0 + jnp.exp(-x))


def _dot(a, b):
    return jnp.dot(a, b, preferred_element_type=F32)


def _lane_onehot(lane_idx):
    return jnp.where(lax.broadcasted_iota(jnp.int32, (1, LANES), 1) == lane_idx, 1.0, 0.0)


def _mem_ones_lane(head):
    return MEM_HD if head % 2 == 0 else 0


def _head_rsqrt_mxu(t, n_valid):
    ones = jnp.where(lax.broadcasted_iota(jnp.int32, (LANES, LANES), 0) < n_valid, 1.0, 0.0)
    ss = _dot((t * t).astype(BF16), ones.astype(BF16))
    return lax.rsqrt(ss * (1.0 / n_valid) + EPS)


def _swap_rope(t):
    return pltpu.roll(t, LANES - HALF_ROPE, 1)


def _mem_kv_kernel(mem_ref, g_ref, w_ref, gk_ref, mkt_ref, mv_ref):
    m = _rms(mem_ref[0], g_ref[...], D_MODEL).astype(BF16)
    kv = _dot(m, w_ref[...])
    for h in range(MEM_HEADS):
        k = _rms(kv[:, h * LANES:(h + 1) * LANES], gk_ref[...], MEM_HD)
        mkt_ref[0, h] = k.T.astype(BF16)
        off = (MEM_HEADS + h) * LANES
        mv_ref[0, h] = (kv[:, off:off + LANES] + _lane_onehot(_mem_ones_lane(h))).astype(BF16)


def _mem_kv(mem, g_mem, w_kv, g_k):
    b, m, _ = mem.shape
    return pl.pallas_call(
        _mem_kv_kernel,
        grid=(b,),
        in_specs=[pl.BlockSpec((1, m, D_MODEL), lambda i: (i, 0, 0)),
                  _const_spec((1, D_MODEL)),
                  _const_spec((D_MODEL, 2 * MEM_HEADS * LANES)),
                  _const_spec((1, LANES))],
        out_specs=[pl.BlockSpec((1, MEM_HEADS, LANES, m), lambda i: (i, 0, 0, 0)),
                   pl.BlockSpec((1, MEM_HEADS, m, LANES), lambda i: (i, 0, 0, 0))],
        out_shape=[jax.ShapeDtypeStruct((b, MEM_HEADS, LANES, m), BF16),
                   jax.ShapeDtypeStruct((b, MEM_HEADS, m, LANES), BF16)],
        compiler_params=_cparams("arbitrary"),
        name="mem_kv",
    )(mem, g_mem, w_kv, g_k)


def _in_proj_kernel(x_ref, aq_ref, bq_ref, ak_ref, bk_ref, mkt_ref, mv_ref,
                    gmix_ref, w1_ref, gq_ref, wuq_ref, gkv_ref, wukv_ref, gmq_ref,
                    q_ref, k_ref, vt_ref, u_ref, om_ref):
    h = _rms(x_ref[0], gmix_ref[...], D_MODEL).astype(BF16)
    z = _dot(h, w1_ref[...])
    u_ref[0] = z[:, Z_U:Z_U + SSM_W]

    low_half = lax.broadcasted_iota(jnp.int32, (1, LANES), 1) < MEM_HD

    def memory_head(hd):
        off = Z_QM + hd * LANES
        mq = z[:, off:off + LANES]
        r = lax.rsqrt(jnp.sum(mq * mq, axis=-1, keepdims=True) * (1.0 / MEM_HD) + EPS)
        s = _dot((mq * gmq_ref[...] * r).astype(BF16), mkt_ref[0, hd])
        p = jnp.exp2(s - jnp.max(s, axis=-1, keepdims=True))
        o = _dot(p.astype(BF16), mv_ref[0, hd])
        one = _mem_ones_lane(hd)
        return o * (1.0 / o[:, one:one + 1])

    def memory_head_pair(pair):
        even, odd = memory_head(2 * pair), memory_head(2 * pair + 1)
        om_ref[0, :, pair * LANES:(pair + 1) * LANES] = jnp.where(low_half, even, odd).astype(BF16)

    cq = _rms(z[:, Z_CQ:Z_CQ + Q_LORA], gq_ref[...], Q_LORA).astype(BF16)
    q_all = _dot(cq, wuq_ref[...])
    ckv = _rms(z[:, Z_CKV:Z_CKV + KV_LORA], gkv_ref[...], KV_LORA).astype(BF16)
    kv = _dot(ckv, wukv_ref[...])

    aq, bq, ak, bk = aq_ref[...], bq_ref[...], ak_ref[...], bk_ref[...]
    k_rope = z[:, Z_KR:Z_KR + LANES]
    k_rope_rot = k_rope * ak + _swap_rope(k_rope) * bk
    real_lane = lax.broadcasted_iota(jnp.int32, (1, LANES), 1) < QK_DIM
    ss_rope = jnp.sum(jnp.where(real_lane, k_rope * k_rope, 0.0), axis=-1, keepdims=True)
    hw = MLA_HEADS * LANES

    for hd in range(MLA_HEADS):
        sl = slice(hd * LANES, (hd + 1) * LANES)
        qh = q_all[:, sl]
        r = _head_rsqrt_mxu(qh, QK_DIM)
        q_ref[0, hd] = ((qh * aq + _swap_rope(qh) * bq) * r).astype(BF16)
        kh = kv[:, sl]
        ss = jnp.sum(kh * kh, axis=-1, keepdims=True) + ss_rope
        r = lax.rsqrt(ss * (1.0 / QK_DIM) + EPS)
        k_ref[0, hd] = ((kh * ak + k_rope_rot) * r).astype(BF16)
        heads_per_pair = 2 * MLA_HEADS // MEM_HEADS
        if hd % heads_per_pair == 0:
            memory_head_pair(hd // heads_per_pair)

    v_t = kv[:, hw:hw + MLA_HEADS * V_HEAD].T.astype(BF16)
    tm = v_t.shape[1]
    ones_rows = jnp.where(lax.broadcasted_iota(jnp.int32, (VT_ROWS - V_HEAD, tm), 0) == 0, 1.0, 0.0)
    for hd in range(MLA_HEADS):
        vt_ref[0, hd, :V_HEAD, :] = v_t[hd * V_HEAD:(hd + 1) * V_HEAD, :]
        vt_ref[0, hd, V_HEAD:, :] = ones_rows.astype(BF16)


def _in_proj(x, rope, mkt, mv, wts):
    b, l, _ = x.shape
    tm = min(IN_PROJ_TILE, l)
    m = mkt.shape[-1]
    tile = lambda last: pl.BlockSpec((1, tm, last), lambda i, j: (i, j, 0))
    head_tile = pl.BlockSpec((1, MLA_HEADS, tm, LANES), lambda i, j: (i, 0, j, 0))
    rope_spec = pl.BlockSpec((tm, LANES), lambda i, j: (j, 0))
    return pl.pallas_call(
        _in_proj_kernel,
        grid=(b, l // tm),
        in_specs=[tile(D_MODEL), rope_spec, rope_spec, rope_spec, rope_spec,
                  pl.BlockSpec((1, MEM_HEADS, LANES, m), lambda i, j: (i, 0, 0, 0)),
                  pl.BlockSpec((1, MEM_HEADS, m, LANES), lambda i, j: (i, 0, 0, 0))]
                 + [_const_spec(w.shape) for w in wts],
        out_specs=[head_tile, head_tile,
                   pl.BlockSpec((1, MLA_HEADS, VT_ROWS, tm), lambda i, j: (i, 0, 0, j)),
                   tile(SSM_W), tile(MEM_W)],
        out_shape=[jax.ShapeDtypeStruct((b, MLA_HEADS, l, LANES), BF16),
                   jax.ShapeDtypeStruct((b, MLA_HEADS, l, LANES), BF16),
                   jax.ShapeDtypeStruct((b, MLA_HEADS, VT_ROWS, l), BF16),
                   jax.ShapeDtypeStruct((b, l, SSM_W), F32),
                   jax.ShapeDtypeStruct((b, l, MEM_W), BF16)],
        compiler_params=_cparams("arbitrary", "arbitrary"),
        name="in_proj",
    )(x, *rope, mkt, mv, *wts)


def _attn_kernel(q_ref, k_ref, vt_ref, o_ref, *, subtract_max):
    outs = []
    for j in range(2):
        s_t = lax.dot_general(k_ref[0, j], q_ref[0, j], (((1,), (1,)), ((), ())),
                              preferred_element_type=F32)
        if subtract_max:
            s_t = s_t - jnp.max(s_t, axis=0, keepdims=True)
        o_t = _dot(vt_ref[0, j], jnp.exp2(s_t).astype(BF16))
        outs.append(o_t[:V_HEAD] * (1.0 / o_t[V_HEAD:V_HEAD + 1]))
    o_ref[0] = jnp.concatenate(outs, axis=0).T.astype(BF16)


def _attention(score_bound, q, k, vt):
    b, _, l, _ = q.shape
    tq = min(ATTN_Q_TILE, l)

    def call(subtract_max):
        return pl.pallas_call(
            functools.partial(_attn_kernel, subtract_max=subtract_max),
            grid=(b, MLA_HEADS // 2, l // tq),
            in_specs=[pl.BlockSpec((1, 2, tq, LANES), lambda i, h, j: (i, h, j, 0)),
                      pl.BlockSpec((1, 2, l, LANES), lambda i, h, j: (i, h, 0, 0)),
                      pl.BlockSpec((1, 2, VT_ROWS, l), lambda i, h, j: (i, h, 0, 0))],
            out_specs=pl.BlockSpec((1, tq, LANES), lambda i, h, j: (i, j, h)),
            out_shape=jax.ShapeDtypeStruct((b, l, MLA_HEADS * V_HEAD), BF16),
            compiler_params=_cparams("arbitrary", "arbitrary", "arbitrary"),
            name="mla_attn_shifted" if subtract_max else "mla_attn",
        )

    return lax.cond(score_bound <= SAFE_SCORE_LOG2, call(False), call(True), q, k, vt)


def _cpow(r, th, e):
    mag = jnp.exp(r * e)
    return mag * jnp.cos(th * e), mag * jnp.sin(th * e)


def _zoh(lr, li, dt):
    a_re, a_im = _cpow(lr * dt, li * dt, 1.0)
    den = lr * lr + li * li
    n_re = a_re - 1.0
    return (n_re * lr + a_im * li) / den, (a_im * lr - n_re * li) / den


def _cmul(ar, ai, br, bi):
    return ar * br - ai * bi, ar * bi + ai * br


def _ssm_prep_kernel(prow_ref, pcol_ref, bt_ref, btt_ref, ctt_ref,
                     m_ref, wst_ref, cp_ref, ap_ref):
    iota = lax.broadcasted_iota
    t_row = jnp.minimum(iota(jnp.int32, (LANES, SSM_STATE), 0), SSM_T - 1).astype(F32)
    t_lane = jnp.minimum(iota(jnp.int32, (SSM_STATE, LANES), 1), SSM_T - 1).astype(F32)
    spread_rows = jnp.where(iota(jnp.int32, (SSM_TW, LANES), 0) // SSM_GROUP
                            == iota(jnp.int32, (SSM_TW, LANES), 1), 1.0, 0.0)
    spread_lanes = jnp.where(iota(jnp.int32, (LANES, SSM_TW), 1) // SSM_GROUP
                             == iota(jnp.int32, (LANES, SSM_TW), 0), 1.0, 0.0)
    exact_dot = lambda a, b: jnp.dot(a, b, preferred_element_type=F32, precision=lax.Precision.HIGHEST)

    def pow_rows(r, th, e):
        p_re, p_im = _cpow(r, th, e)
        return exact_dot(spread_rows, p_re), exact_dot(spread_rows, p_im)

    def pow_lanes(r, th, e):
        p_re, p_im = _cpow(r, th, e)
        return exact_dot(p_re, spread_lanes), exact_dot(p_im, spread_lanes)

    lane = lax.broadcasted_iota(jnp.int32, (SSM_GROUP, SSM_TW), 1)
    k_row = lax.broadcasted_iota(jnp.int32, (2 * SSM_SCAN_STEPS + 2, LANES), 0)
    chunk_pow = (SSM_T * jnp.left_shift(1, k_row // 2)).astype(F32)
    odd_row = (k_row % 2) == 1
    lane_ap = lax.broadcasted_iota(jnp.int32, (2 * SSM_SCAN_STEPS + 2, LANES), 1)

    toeplitz = []
    for d in range(2):
        lr2, li2 = prow_ref[0, 3 * d:3 * d + 1, :], prow_ref[0, 3 * d + 1:3 * d + 2, :]
        dt2 = jnp.exp(prow_ref[0, 3 * d + 2:3 * d + 3, :])
        lr, li, dt = lr2[:, :SSM_STATE], li2[:, :SSM_STATE], dt2[:, :SSM_STATE]
        z_re, z_im = _zoh(lr, li, dt)

        p_re, p_im = _cpow(lr2 * dt2, li2 * dt2, chunk_pow)
        ap_ref[0, d] = jnp.where(odd_row, jnp.where(lane_ap < SSM_STATE, -p_im, p_im), p_re)

        e_in = (SSM_T - 1.0) - t_row if d == 0 else t_row
        w_re, w_im = _cmul(*pow_rows(lr * dt, li * dt, e_in), z_re, z_im)
        b_re, b_im = btt_ref[0, d, 0], btt_ref[0, d, 1]
        s_re, s_im = _cmul(w_re, w_im, b_re, b_im)
        wst_ref[0, :, 2 * d * SSM_STATE:(2 * d + 1) * SSM_STATE] = s_re.astype(BF16)
        wst_ref[0, :, (2 * d + 1) * SSM_STATE:(2 * d + 2) * SSM_STATE] = s_im.astype(BF16)

        lrc, lic = pcol_ref[0, :, 3 * d:3 * d + 1], pcol_ref[0, :, 3 * d + 1:3 * d + 2]
        dtc = jnp.exp(pcol_ref[0, :, 3 * d + 2:3 * d + 3])
        c_re, c_im = ctt_ref[0, d, 0], ctt_ref[0, d, 1]

        e_out = t_lane + 1.0 if d == 0 else SSM_T - t_lane
        g_re, g_im = _cmul(*pow_lanes(lrc * dtc, lic * dtc, e_out), c_re, c_im)
        cp_ref[0, 2 * d * SSM_STATE:(2 * d + 1) * SSM_STATE, :] = g_re.astype(BF16)
        cp_ref[0, (2 * d + 1) * SSM_STATE:(2 * d + 2) * SSM_STATE, :] = (-g_im).astype(BF16)

        e_lag = t_lane if d == 0 else (SSM_T - 1.0) - t_lane
        k_re, k_im = _cmul(*pow_lanes(lrc * dtc, lic * dtc, e_lag), c_re, c_im)
        bb_re, bb_im = _cmul(z_re, z_im, bt_ref[0, d, 0], bt_ref[0, d, 1])
        lag = (jnp.dot(bb_re, k_re, preferred_element_type=F32, precision=lax.Precision.HIGHEST)
               - jnp.dot(bb_im, k_im, preferred_element_type=F32, precision=lax.Precision.HIGHEST))
        toeplitz.append(lag)

    lag_f, lag_b = toeplitz
    for s in range(SSM_T):
        fwd = lag_f if s == 0 else pltpu.roll(lag_f, SSM_GROUP * s, 1)
        fwd = jnp.where(lane >= SSM_GROUP * s, fwd, 0.0)
        shift = (SSM_TW - SSM_GROUP * (SSM_T - 1 - s)) % SSM_TW
        bwd = lag_b if shift == 0 else pltpu.roll(lag_b, shift, 1)
        bwd = jnp.where(lane < SSM_GROUP * (s + 1), bwd, 0.0)
        m_ref[0, s * SSM_GROUP:(s + 1) * SSM_GROUP, :] = (fwd + bwd).astype(BF16)


def _ssm_prep(lam_re, lam_im, log_step, b_re, b_im, c_re, c_im):
    g, p, hg, t = SSM_GROUPS, SSM_STATE, SSM_GROUP, SSM_T
    step = jnp.broadcast_to(log_step[:, :, None], (2, g, p))
    rows = jnp.stack([lam_re[0], lam_im[0], step[0], lam_re[1], lam_im[1], step[1]], axis=1)
    rows = jnp.pad(rows, ((0, 0), (0, 2), (0, 0)))
    prow = jnp.concatenate([rows, rows], axis=-1)
    pcol = rows.transpose(0, 2, 1)
    bt = jnp.stack([b_re, b_im], axis=1).transpose(2, 0, 1, 4, 3)
    btt = jnp.tile(bt, (1, 1, 1, t, 1))
    ct = jnp.stack([c_re, c_im], axis=1).transpose(2, 0, 1, 4, 3)
    ctt = jnp.tile(ct, (1, 1, 1, 1, t))
    n_ap = 2 * SSM_SCAN_STEPS + 2
    blk = lambda *s: pl.BlockSpec((1,) + s, lambda i: (i,) + (0,) * len(s))
    return pl.pallas_call(
        _ssm_prep_kernel,
        grid=(g,),
        in_specs=[blk(8, LANES), blk(p, 8), blk(2, 2, hg, p), blk(2, 2, t * hg, p), blk(2, 2, p, t * hg)],
        out_specs=[blk(SSM_TW, SSM_TW), blk(SSM_TW, 4 * p), blk(4 * p, SSM_TW), blk(2, n_ap, LANES)],
        out_shape=[jax.ShapeDtypeStruct((g, SSM_TW, SSM_TW), BF16),
                   jax.ShapeDtypeStruct((g, SSM_TW, 4 * p), BF16),
                   jax.ShapeDtypeStruct((g, 4 * p, SSM_TW), BF16),
                   jax.ShapeDtypeStruct((g, 2, n_ap, LANES), F32)],
        compiler_params=_cparams("arbitrary"),
        name="ssm_prep",
    )(prow, pcol, bt, btt, ctt)


def _atom_transpose_perm():
    i = jnp.arange(SSM_PERM_W)
    a, b, h = i // LANES, (i % LANES) // SSM_GROUP, i % SSM_GROUP
    j = b * LANES + a * SSM_GROUP + h
    return (j[:, None] == i[None, :]).astype(BF16)


def _ssm_kernel(u_lo_ref, u_hi_ref, perm_ref, m_ref, wst_ref, cp_ref, ap_ref, y_lo_ref, y_hi_ref,
                ug_ref, yg_ref, z_ref, s_ref):
    u_refs, y_refs = (u_lo_ref, u_hi_ref), (y_lo_ref, y_hi_ref)
    nc = ug_ref.shape[1]
    perm = perm_ref[...]
    blocks = [(tb, half) for tb in range(SSM_T // SSM_PER_TILE) for half in range(SSM_GROUPS // SSM_PER_TILE)]

    def token_rows(refs, tb, t, half):
        return refs[half].at[0, pl.ds(tb * SSM_PER_TILE + t, nc, stride=SSM_T), :]

    for tb, half in blocks:
        x = jnp.concatenate([token_rows(u_refs, tb, t, half)[...] for t in range(SSM_PER_TILE)], axis=1)
        r = _dot(x.astype(BF16), perm).astype(BF16)
        for gi in range(SSM_PER_TILE):
            ug_ref[half * SSM_PER_TILE + gi, :, tb * LANES:(tb + 1) * LANES] = r[:, gi * LANES:(gi + 1) * LANES]

    for g in range(SSM_GROUPS):
        z_ref[g] = _dot(ug_ref[g], wst_ref[g])

    rows = SSM_GROUPS * nc
    chunk = lax.broadcasted_iota(jnp.int32, (rows, LANES), 0) & (nc - 1)

    def shifted(s3, sh, d):
        flat = s3.reshape(rows, LANES)
        if d == 0:
            return jnp.where(chunk >= sh, pltpu.roll(flat, sh, 0), 0.0)
        return jnp.where(chunk < nc - sh, pltpu.roll(flat, rows - sh, 0), 0.0)

    as3d = lambda a: a.reshape(SSM_GROUPS, nc, LANES)
    for d in range(2):
        s3 = z_ref[:, :, d * LANES:(d + 1) * LANES]
        for k in range(nc.bit_length() - 1):
            t = shifted(s3, 1 << k, d)
            s3 = s3 + ap_ref[:, d, 2 * k:2 * k + 1, :] * as3d(t) \
                + ap_ref[:, d, 2 * k + 1:2 * k + 2, :] * as3d(pltpu.roll(t, SSM_STATE, 1))
        s_ref[:, :, d * LANES:(d + 1) * LANES] = as3d(shifted(s3, 1, d)).astype(BF16)

    for g in range(SSM_GROUPS):
        yg_ref[g] = _dot(ug_ref[g], m_ref[g]) + _dot(s_ref[g], cp_ref[g])

    for tb, half in blocks:
        yv = jnp.concatenate([yg_ref[half * SSM_PER_TILE + gi, :, tb * LANES:(tb + 1) * LANES]
                              for gi in range(SSM_PER_TILE)], axis=1)
        hi = yv.astype(BF16)
        lo = (yv - hi.astype(F32)).astype(BF16)
        out = _dot(hi, perm) + _dot(lo, perm)
        for t in range(SSM_PER_TILE):
            token_rows(y_refs, tb, t, half)[...] = out[:, t * LANES:(t + 1) * LANES]


def _ssm(u, tables):
    m, wst, cp, ap = tables
    b, l, w = u.shape
    nc = l // SSM_T
    assert nc & (nc - 1) == 0 and nc <= 1 << SSM_SCAN_STEPS, "chunk count must be a power of two"
    perm = _atom_transpose_perm()
    assert w == 2 * LANES
    half = lambda j: pl.BlockSpec((1, l, LANES), lambda i: (i, 0, j))
    return pl.pallas_call(
        _ssm_kernel,
        grid=(b,),
        in_specs=[half(0), half(1)] + [_const_spec(a.shape) for a in (perm, m, wst, cp, ap)],
        out_specs=[half(0), half(0)],
        out_shape=[jax.ShapeDtypeStruct((b, l, LANES), F32)] * 2,
        scratch_shapes=[pltpu.VMEM((SSM_GROUPS, nc, SSM_TW), BF16),
                        pltpu.VMEM((SSM_GROUPS, nc, SSM_TW), F32),
                        pltpu.VMEM((SSM_GROUPS, nc, 2 * LANES), F32),
                        pltpu.VMEM((SSM_GROUPS, nc, 2 * LANES), BF16)],
        compiler_params=_cparams("arbitrary"),
        name="ssm_scan",
    )(u, u, perm, m, wst, cp, ap)


def _gelu_tanh(x):
    return 0.5 * x * (1.0 + jnp.tanh(math.sqrt(2.0 / math.pi) * (x + 0.044715 * (x * x * x))))


def _merge_kernel(x_ref, oa_ref, y_lo_ref, y_hi_ref, u_ref, om_ref,
                  gmix_ref, wg_ref, d_ref, wglu_ref, wa_ref, ws_ref, wm_ref, wout_ref, o_ref):
    x = x_ref[...]
    h = _rms(x, gmix_ref[...], D_MODEL).astype(BF16)
    y = jnp.concatenate([y_lo_ref[...], y_hi_ref[...]], axis=1)
    y = _gelu_tanh(y + d_ref[...] * u_ref[...])
    o_ssm = (y * _sigmoid(_dot(y.astype(BF16), wglu_ref[...]))).astype(BF16)
    branches = ((oa_ref[...], wa_ref), (o_ssm, ws_ref), (om_ref[...], wm_ref))
    merged = None
    for i, (o_b, w_ref) in enumerate(branches):
        gate = _sigmoid(_dot(h, wg_ref[:, i * D_MODEL:(i + 1) * D_MODEL]))
        term = gate * _dot(o_b, w_ref[...])
        merged = term if merged is None else merged + term
    o_ref[...] = x + _dot(merged.astype(BF16), wout_ref[...])


def _merge(x2, oa2, y_lo, y_hi, u2, om2, wts):
    n = x2.shape[0]
    tm = min(2 * TOKEN_TILE, n)
    tile = lambda a: pl.BlockSpec((tm, a.shape[1]), lambda i: (i, 0))
    return pl.pallas_call(
        _merge_kernel,
        grid=(n // tm,),
        in_specs=[tile(x2), tile(oa2), tile(y_lo), tile(y_hi), tile(u2), tile(om2)]
                 + [_const_spec(w.shape) for w in wts],
        out_specs=tile(x2),
        out_shape=jax.ShapeDtypeStruct(x2.shape, F32),
        compiler_params=_cparams("arbitrary"),
        name="merge_out",
    )(x2, oa2, y_lo, y_hi, u2, om2, *wts)


def _ffn_kernel(x_ref, g_ref, wgate_ref, wup_ref, wdown_ref, o_ref):
    x = x_ref[...]
    h = _rms(x, g_ref[...], D_MODEL).astype(BF16)
    gate = _dot(h, wgate_ref[...])
    act = (gate * _sigmoid(gate) * _dot(h, wup_ref[...])).astype(BF16)
    o_ref[...] = x + _dot(act, wdown_ref[...])


def _ffn(x2, wts):
    n = x2.shape[0]
    tm = TOKEN_TILE
    tile = pl.BlockSpec((tm, D_MODEL), lambda i: (i, 0))
    return pl.pallas_call(
        _ffn_kernel,
        grid=(n // tm,),
        in_specs=[tile] + [_const_spec(w.shape) for w in wts],
        out_specs=tile,
        out_shape=jax.ShapeDtypeStruct(x2.shape, F32),
        compiler_params=_cparams("arbitrary"),
        name="ffn",
    )(x2, *wts)


def _pad_heads(w, heads, width):
    lead = w.shape[:-1]
    w = w.reshape(lead + (heads, width))
    w = jnp.pad(w, [(0, 0)] * len(lead) + [(0, 0), (0, LANES - width)])
    return w.reshape(lead + (heads * LANES,))


def _dup_first_rope_half(w):
    return jnp.concatenate([w[..., :QK_DIM], w[..., ROPE_LO:ROPE_HI],
                            w[..., QK_DIM + HALF_ROPE:]], axis=-1)


def _pad_row(g):
    return jnp.pad(g, (0, LANES - g.shape[0])).reshape(1, LANES)


def _rope_tables(length, gain, scale):
    inv = 1.0 / (ROPE_THETA ** (jnp.arange(0, QK_ROPE, 2, dtype=F32) / QK_ROPE))
    ang = jnp.arange(length, dtype=F32)[:, None] * inv[None, :]
    cos, sin = jnp.cos(ang), jnp.sin(ang)
    g = gain * scale
    g_nope, g_lo, g_hi = g[:QK_NOPE], g[ROPE_LO:ROPE_HI], g[ROPE_HI:QK_DIM]
    pad = jnp.zeros((length, LANES - QK_DIM), F32)
    a = jnp.concatenate([jnp.broadcast_to(g_nope, (length, QK_NOPE)), cos * g_lo, cos * g_hi, pad], axis=1)
    b = jnp.concatenate([jnp.zeros((length, QK_NOPE), F32), -sin * g_hi, sin * g_lo, pad], axis=1)
    return a, b


def _layer(x, mem, rope, score_bound, p):
    b, l, d = x.shape
    mkt, mv = _mem_kv(mem, p["g_mem"], p["w_mem_kv"], p["g_mk"])
    q, k, vt, u, o_mem = _in_proj(x, rope, mkt, mv, p["in_proj"])
    o_attn = _attention(score_bound, q, k, vt)

    y_lo, y_hi = (y.reshape(b * l, LANES) for y in _ssm(u, p["ssm"]))

    x2 = x.reshape(b * l, d)
    x1 = _merge(x2, o_attn.reshape(b * l, -1), y_lo, y_hi, u.reshape(b * l, SSM_W),
                o_mem.reshape(b * l, -1), p["merge"])
    return _ffn(x1, p["ffn"]).reshape(b, l, d)


def kernel(x_prompt, x_sample, mem_prompt, mem_sample, norm_mix_g, norm_mem_g, w_in, q_lora_norm_g, kv_lora_norm_g, w_uq, w_ukv, mla_q_norm_g, mla_k_norm_g, ssm_lambda_re, ssm_lambda_im, ssm_log_step, ssm_b_re, ssm_b_im, ssm_c_re, ssm_c_im, ssm_d, ssm_w_glu, w_mem_kv, mem_q_norm_g, mem_k_norm_g, w_br_attn, w_br_ssm, w_br_mem, w_out, norm_ffn_g, w_ffn_gate, w_ffn_up, w_ffn_down):
    assert norm_mix_g.shape[0] == 1, "single-layer encoder"
    row = lambda g: g.reshape(1, -1)
    w_in0 = w_in[0]

    k_rope_cols = jnp.pad(w_in0[:, OFF_KV:OFF_KR], ((0, 0), (QK_NOPE, LANES - QK_DIM)))
    w1 = jnp.concatenate([w_in0[:, :OFF_KV], _dup_first_rope_half(k_rope_cols),
                          w_in0[:, OFF_KR:OFF_SSM],
                          _pad_heads(w_in0[:, OFF_SSM:OFF_MEM], MEM_HEADS, MEM_HD)], axis=1).astype(BF16)
    w_uq_p = _pad_heads(w_uq[0], MLA_HEADS, QK_DIM).reshape(Q_LORA, MLA_HEADS, LANES)
    w_uq_p = _dup_first_rope_half(w_uq_p).reshape(Q_LORA, -1).astype(BF16)
    w_ukv_h = w_ukv[0].reshape(KV_LORA, MLA_HEADS, QK_NOPE + V_HEAD)
    w_k = _pad_heads(w_ukv_h[..., :QK_NOPE].reshape(KV_LORA, -1), MLA_HEADS, QK_NOPE)
    w_v = w_ukv_h[..., QK_NOPE:].reshape(KV_LORA, -1)
    w_ukv_p = jnp.concatenate([w_k, w_v], axis=1).astype(BF16)

    w_mkv = w_mem_kv[0]
    w_mv = _pad_heads(w_mkv[:, MEM_W:], MEM_HEADS, MEM_HD).reshape(D_MODEL, MEM_HEADS, LANES)
    w_mv = jnp.where((jnp.arange(MEM_HEADS) % 2 == 1)[None, :, None], jnp.roll(w_mv, MEM_HD, axis=-1), w_mv)
    w_mkv_p = jnp.concatenate([_pad_heads(w_mkv[:, :MEM_W], MEM_HEADS, MEM_HD),
                               w_mv.reshape(D_MODEL, -1)], axis=1).astype(BF16)

    params = {
        "g_mem": row(norm_mem_g[0]), "w_mem_kv": w_mkv_p, "g_mk": _pad_row(mem_k_norm_g[0]),
        "in_proj": (row(norm_mix_g[0]), w1, row(q_lora_norm_g[0]), w_uq_p, row(kv_lora_norm_g[0]), w_ukv_p,
                    _pad_row(mem_q_norm_g[0] * MEM_Q_SCALE)),
        "ssm": _ssm_prep(ssm_lambda_re[0], ssm_lambda_im[0], ssm_log_step[0],
                         ssm_b_re[0], ssm_b_im[0], ssm_c_re[0], ssm_c_im[0]),
        "merge": (row(norm_mix_g[0]), w_in0[:, OFF_MEM:].astype(BF16), row(ssm_d[0]),
                  ssm_w_glu[0].astype(BF16), w_br_attn[0].astype(BF16), w_br_ssm[0].astype(BF16),
                  w_br_mem[0].astype(BF16), w_out[0].astype(BF16)),
        "ffn": (row(norm_ffn_g[0]), w_ffn_gate[0].astype(BF16), w_ffn_up[0].astype(BF16),
                w_ffn_down[0].astype(BF16)),
    }
    score_bound = (1.02 * QK_DIM * Q_SCALE * jnp.max(jnp.abs(mla_q_norm_g[0]))
                   * jnp.max(jnp.abs(mla_k_norm_g[0]))).astype(F32)

    def rope(length):
        return (_rope_tables(length, mla_q_norm_g[0], Q_SCALE)
                + _rope_tables(length, mla_k_norm_g[0], 1.0))

    rope_p = rope(x_prompt.shape[1])
    rope_s = rope_p if x_sample.shape[1] == x_prompt.shape[1] else rope(x_sample.shape[1])
    y_prompt = _layer(x_prompt, mem_prompt, rope_p, score_bound, params)
    y_sample = _layer(x_sample, mem_sample, rope_s, score_bound, params)
    return (y_prompt, y_sample)
```
